```python
import jax, jax.numpy as jnp
from jax import lax
import numpy as np

D_MODEL = 2048
BATCH = 4
SEQ = 2048
DEPTH = 2

CHUNK = 64
LRU_WIDTH = D_MODEL // 2
LRU_BLOCKS = 8
LRU_BLOCK = LRU_WIDTH // LRU_BLOCKS
LRU_CONV = 4
LRU_C = 8.0
ATT_HEADS = 8
ATT_HEAD_DIM = 128
ATT_WIDTH = ATT_HEADS * ATT_HEAD_DIM
Q_BLOCK = 128
CONV_WIDTH = D_MODEL // 2
CONV_K = 3
N_BRANCH = 3
IN_WIDTHS = (LRU_WIDTH, LRU_WIDTH, ATT_WIDTH, ATT_WIDTH, ATT_WIDTH, ATT_HEADS, CONV_WIDTH, CONV_WIDTH, CONV_WIDTH)
D_IN = 2 * LRU_WIDTH + 3 * ATT_WIDTH + ATT_HEADS + 3 * CONV_WIDTH
D_FF = 5504
N_EXPERTS = 8
TOP_K = 2
D_FF_EXPERT = 7168
N_DENSE = (DEPTH + 1) // 2
N_MOE = DEPTH // 2
EPS = 1e-6

kernel_name = "hybrid_rglru_fox_shortconv_moe_block"

F32 = jnp.float32


def rms_norm(x, gain):
    x32 = x.astype(F32)
    y = x32 * lax.rsqrt(jnp.mean(x32 * x32, axis=-1, keepdims=True) + EPS)
    return (y * gain.astype(F32)).astype(x.dtype)


def causal_depthwise_conv(x, w):
    k, c = w.shape
    return lax.conv_general_dilated(
        x, w[:, None, :].astype(x.dtype), window_strides=(1,), padding=[(k - 1, 0)],
        dimension_numbers=("NWC", "WIO", "NWC"), feature_group_count=c)


def rg_lru_branch(xb, gb, conv_w, conv_b, w_a, b_a, w_x, b_x, lam):
    xc = causal_depthwise_conv(xb, conv_w) + conv_b
    bn, s, w = xc.shape
    xg = xc.reshape(bn, s, LRU_BLOCKS, LRU_BLOCK)
    r = jax.nn.sigmoid(jnp.einsum("bsnc,ncd->bsnd", xg, w_a).reshape(bn, s, w) + b_a)
    i = jax.nn.sigmoid(jnp.einsum("bsnc,ncd->bsnd", xg, w_x).reshape(bn, s, w) + b_x)
    log_a = -LRU_C * r.astype(F32) * jax.nn.softplus(-lam.astype(F32))
    a = jnp.exp(log_a)
    mult = jnp.sqrt(-jnp.expm1(2.0 * log_a))
    b = mult * (i * xc).astype(F32)

    def combine(lhs, rhs):
        a1, b1 = lhs
        a2, b2 = rhs
        return a1 * a2, a2 * b1 + b2

    _, h = lax.associative_scan(combine, (a, b), axis=1)
    return h.astype(xb.dtype) * jax.nn.gelu(gb)


def forgetting_attention(q, k, v, f_logit, b_f, q_gain, k_gain):
    bn, s, _ = q.shape

    def heads(t):
        return t.reshape(bn, s, ATT_HEADS, ATT_HEAD_DIM).transpose(0, 2, 1, 3)

    qh = rms_norm(heads(q), q_gain)
    kh = rms_norm(heads(k), k_gain)
    vh = heads(v)
    log_f = jax.nn.log_sigmoid((f_logit + b_f).astype(F32))
    cum = jnp.cumsum(log_f, axis=1).transpose(0, 2, 1)
    scale = ATT_HEAD_DIM ** -0.5
    outs = []
    for qb in range(s // Q_BLOCK):
        qs, qe = qb * Q_BLOCK, (qb + 1) * Q_BLOCK
        sc = jnp.einsum("bhqd,bhkd->bhqk", qh[:, :, qs:qe], kh[:, :, :qe]).astype(F32) * scale
        sc = sc + (cum[:, :, qs:qe, None] - cum[:, :, None, :qe])
        mask = jnp.arange(qs, qe)[:, None] >= jnp.arange(qe)[None, :]
        sc = jnp.where(mask, sc, -jnp.inf)
        p = jax.nn.softmax(sc, axis=-1).astype(vh.dtype)
        outs.append(jnp.einsum("bhqk,bhkd->bhqd", p, vh[:, :, :qe]))
    o = jnp.concatenate(outs, axis=2)
    return o.transpose(0, 2, 1, 3).reshape(bn, s, ATT_WIDTH)


def short_conv_branch(gate_b, gate_c, hb, conv_w):
    return gate_b * causal_depthwise_conv(gate_c * hb, conv_w)


def swiglu(h, wg, wu, wd):
    return (jax.nn.silu(h @ wg) * (h @ wu)) @ wd


def moe_swiglu(h, w_router, b_router, wg, wu, wd):
    bn, s, d = h.shape
    t = h.reshape(-1, d)
    logits = (t @ w_router).astype(F32) + b_router.astype(F32)
    top_v, top_i = lax.top_k(logits, TOP_K)
    top_w = jax.nn.softmax(top_v, axis=-1)
    gates = jnp.sum(jax.nn.one_hot(top_i, N_EXPERTS, dtype=F32) * top_w[..., None], axis=1).astype(h.dtype)
    out = jnp.zeros_like(t)
    for e in range(N_EXPERTS):
        out = out + gates[:, e:e + 1] * swiglu(t, wg[e], wu[e], wd[e])
    return out.reshape(bn, s, d)


def setup_inputs(seed: int = 0) -> dict:
    key = jax.random.key(seed)
    ks = jax.random.split(key, 40)

    def nrm(k, shape, fan_in):
        return jax.random.normal(k, shape, F32) * (fan_in ** -0.5)

    def gain(k, shape):
        return 1.0 + 0.05 * jax.random.normal(k, shape, F32)

    def small(k, shape, s=0.02):
        return s * jax.random.normal(k, shape, F32)

    a_c = jax.random.uniform(ks[8], (DEPTH, LRU_WIDTH), F32, 0.9, 0.999)
    a0 = a_c ** (1.0 / LRU_C)
    lru_lambda = jnp.log(a0) - jnp.log1p(-a0)
    return {
        "x": jax.random.normal(ks[0], (BATCH, SEQ, D_MODEL), F32),
        "norm_mix_g": gain(ks[1], (DEPTH, D_MODEL)),
        "w_in": nrm(ks[2], (DEPTH, D_MODEL, D_IN), D_MODEL),
        "lru_conv_w": nrm(ks[3], (DEPTH, LRU_CONV, LRU_WIDTH), LRU_CONV),
        "lru_conv_b": small(ks[4], (DEPTH, LRU_WIDTH)),
        "lru_wa": nrm(ks[5], (DEPTH, LRU_BLOCKS, LRU_BLOCK, LRU_BLOCK), LRU_BLOCK),
        "lru_ba": small(ks[6], (DEPTH, LRU_WIDTH)),
        "lru_wx": nrm(ks[7], (DEPTH, LRU_BLOCKS, LRU_BLOCK, LRU_BLOCK), LRU_BLOCK),
        "lru_bx": small(ks[9], (DEPTH, LRU_WIDTH)),
        "lru_lambda": lru_lambda,
        "fox_bf": 3.0 + 0.5 * jax.random.normal(ks[10], (DEPTH, ATT_HEADS), F32),
        "q_norm_g": gain(ks[11], (DEPTH, ATT_HEAD_DIM)),
        "k_norm_g": gain(ks[12], (DEPTH, ATT_HEAD_DIM)),
        "sc_conv_w": nrm(ks[13], (DEPTH, CONV_K, CONV_WIDTH), CONV_K),
        "w_branch_lru": nrm(ks[14], (DEPTH, LRU_WIDTH, D_MODEL), LRU_WIDTH),
        "w_branch_att": nrm(ks[15], (DEPTH, ATT_WIDTH, D_MODEL), ATT_WIDTH),
        "w_branch_conv": nrm(ks[16], (DEPTH, CONV_WIDTH, D_MODEL), CONV_WIDTH),
        "w_merge": nrm(ks[17], (DEPTH, D_MODEL, N_BRANCH * D_MODEL), D_MODEL),
        "b_merge": small(ks[18], (DEPTH, N_BRANCH * D_MODEL)),
        "w_out": nrm(ks[19], (DEPTH, D_MODEL, D_MODEL), D_MODEL),
        "norm_ffn_g": gain(ks[20], (DEPTH, D_MODEL)),
        "ffn_wg": nrm(ks[21], (N_DENSE, D_MODEL, D_FF), D_MODEL),
        "ffn_wu": nrm(ks[22], (N_DENSE, D_MODEL, D_FF), D_MODEL),
        "ffn_wd": nrm(ks[23], (N_DENSE, D_FF, D_MODEL), D_FF),
        "router_w": nrm(ks[24], (N_MOE, D_MODEL, N_EXPERTS), D_MODEL),
        "router_b": small(ks[25], (N_MOE, N_EXPERTS), 0.01),
        "moe_wg": nrm(ks[26], (N_MOE, N_EXPERTS, D_MODEL, D_FF_EXPERT), D_MODEL),
        "moe_wu": nrm(ks[27], (N_MOE, N_EXPERTS, D_MODEL, D_FF_EXPERT), D_MODEL),
        "moe_wd": nrm(ks[28], (N_MOE, N_EXPERTS, D_FF_EXPERT, D_MODEL), D_FF_EXPERT),
    }


def reference(x, norm_mix_g, w_in, lru_conv_w, lru_conv_b, lru_wa, lru_ba, lru_wx, lru_bx, lru_lambda,
              fox_bf, q_norm_g, k_norm_g, sc_conv_w, w_branch_lru, w_branch_att, w_branch_conv,
              w_merge, b_merge, w_out, norm_ffn_g, ffn_wg, ffn_wu, ffn_wd, router_w, router_b,
              moe_wg, moe_wu, moe_wd):
    bn, s, d = x.shape
    split_points = np.cumsum(np.array(IN_WIDTHS))[:-1]
    for l in range(DEPTH):
        u = rms_norm(x, norm_mix_g[l])
        z = u @ w_in[l]
        lru_x, lru_g, q, k, v, f_logit, sc_b, sc_c, sc_h = jnp.split(z, split_points, axis=-1)
        y_lru = rg_lru_branch(lru_x, lru_g, lru_conv_w[l], lru_conv_b[l], lru_wa[l], lru_ba[l],
                              lru_wx[l], lru_bx[l], lru_lambda[l])
        y_att = forgetting_attention(q, k, v, f_logit, fox_bf[l], q_norm_g[l], k_norm_g[l])
        y_conv = short_conv_branch(sc_b, sc_c, sc_h, sc_conv_w[l])
        g = jax.nn.sigmoid(u @ w_merge[l] + b_merge[l]).reshape(bn, s, N_BRANCH, d)
        merged = (g[:, :, 0] * (y_lru @ w_branch_lru[l])
                  + g[:, :, 1] * (y_att @ w_branch_att[l])
                  + g[:, :, 2] * (y_conv @ w_branch_conv[l]))
        x = x + merged @ w_out[l]
        h = rms_norm(x, norm_ffn_g[l])
        if l % 2 == 0:
            x = x + swiglu(h, ffn_wg[l // 2], ffn_wu[l // 2], ffn_wd[l // 2])
        else:
            m = l // 2
            x = x + moe_swiglu(h, router_w[m], router_b[m], moe_wg[m], moe_wu[m], moe_wd[m])
    return x
```

```python
import functools

import jax
import jax.numpy as jnp
from jax import lax
from jax.experimental import pallas as pl
from jax.experimental.pallas import tpu as pltpu

F32 = jnp.float32
BF16 = jnp.bfloat16

D_MODEL = 2048
LRU_WIDTH = 1024
LRU_BLOCKS = 8
LRU_BLOCK = 128
LRU_CONV = 4
LRU_C = 8.0
ATT_HEADS = 8
ATT_HEAD_DIM = 128
ATT_WIDTH = 1024
CONV_WIDTH = 1024
CONV_K = 3
N_EXPERTS = 8
TOP_K = 2
EPS = 1e-6
COL_MAIN = 2 * LRU_WIDTH + 3 * ATT_WIDTH
COL_F = COL_MAIN
COL_TAIL = COL_MAIN + ATT_HEADS

LANES = 128
SUBLANES = 8
V7X_VMEM_BUDGET = 56 * 1024 * 1024

ROW_TILE = 512
SEQ_TILE = 256
Q_TILE = 256
MOE_TILE = 512
GATHER_TILE = 256


def _vmem_limit(block_bytes, temp_bytes=0):
    need = 2 * sum(block_bytes) + temp_bytes + (4 << 20)
    return int(min(max(need, 16 << 20), V7X_VMEM_BUDGET))


def _nbytes(shape, dtype):
    n = 1
    for s in shape:
        n *= s
    return n * jnp.dtype(dtype).itemsize


def _softplus(y):
    return jnp.maximum(y, 0.0) + jnp.log1p(jnp.exp(-jnp.abs(y)))


def _sigmoid(y):
    return 1.0 / (1.0 + jnp.exp(-y))


def _rmsnorm_body(x_ref, g_ref, o_ref):
    x = x_ref[...]
    y = x * lax.rsqrt(jnp.mean(x * x, axis=-1, keepdims=True) + EPS)
    o_ref[...] = (y * g_ref[...]).astype(o_ref.dtype)


def _rmsnorm(x2, gain, l):
    n, d = x2.shape
    return pl.pallas_call(
        _rmsnorm_body,
        grid=(n // ROW_TILE,),
        in_specs=[pl.BlockSpec((ROW_TILE, d), lambda i: (i, 0)),
                  pl.BlockSpec((None, 1, d), lambda i: (l, 0, 0))],
        out_specs=pl.BlockSpec((ROW_TILE, d), lambda i: (i, 0)),
        out_shape=jax.ShapeDtypeStruct((n, d), BF16),
        compiler_params=pltpu.CompilerParams(dimension_semantics=("arbitrary",)),
        name="rmsnorm",
    )(x2, gain.reshape(gain.shape[0], 1, d))


def _mm_body(*refs, pairs, n_a, n_w, n_e, epilogue):
    a_refs = refs[:n_a]
    w_refs = refs[n_a:n_a + n_w]
    e_refs = refs[n_a + n_w:n_a + n_w + n_e]
    o_ref = refs[-1]
    accs = [jnp.dot(a_refs[ai][...], w_refs[wi][...].astype(BF16), preferred_element_type=F32)
            for ai, wi in pairs]
    o_ref[...] = epilogue(accs, [e[...] for e in e_refs]).astype(o_ref.dtype)


def _mm(name, m, n, bm, bn, a_ops, w_ops, e_ops, pairs, epilogue, out_dtype):
    arrays = [a for a, _ in a_ops] + [w for w, _ in w_ops] + [e for e, _ in e_ops]
    specs = [s for _, s in a_ops] + [s for _, s in w_ops] + [s for _, s in e_ops]
    blocks = []
    for arr, spec in a_ops + w_ops + e_ops:
        shp = [s for s in spec.block_shape if s is not None]
        blocks.append(_nbytes(shp, arr.dtype))
    blocks.append(_nbytes((bm, bn), out_dtype))
    w_bf16 = sum(_nbytes([s for s in spec.block_shape if s is not None], BF16) for _, spec in w_ops)
    temps = w_bf16 + len(pairs) * _nbytes((bm, bn), F32)
    body = functools.partial(_mm_body, pairs=tuple(pairs), n_a=len(a_ops), n_w=len(w_ops),
                             n_e=len(e_ops), epilogue=epilogue)
    return pl.pallas_call(
        body,
        grid=(pl.cdiv(n, bn), m // bm),
        in_specs=specs,
        out_specs=pl.BlockSpec((bm, bn), lambda j, i: (i, j)),
        out_shape=jax.ShapeDtypeStruct((m, n), out_dtype),
        compiler_params=pltpu.CompilerParams(
            dimension_semantics=("arbitrary", "arbitrary"),
            vmem_limit_bytes=_vmem_limit(blocks, temps)),
        name=name,
    )(*arrays)


def _a_spec(bm, k):
    return pl.BlockSpec((bm, k), lambda j, i: (i, 0))


def _w_spec(k, bn, l, col_block0=0):
    return pl.BlockSpec((None, k, bn), lambda j, i: (l, 0, j + col_block0))


def _row_spec(bn, l, col_block0=0):
    return pl.BlockSpec((None, 1, bn), lambda j, i: (l, 0, j + col_block0))


def _ep_identity(accs, extras):
    return accs[0]


def _ep_residual(accs, extras):
    return extras[0] + accs[0]


def _ep_swiglu(accs, extras):
    g, u = accs
    return (g * _sigmoid(g)) * u


def _ep_merge(accs, extras):
    g0, g1, g2, p0, p1, p2 = accs
    b0, b1, b2 = extras
    return _sigmoid(g0 + b0) * p0 + _sigmoid(g1 + b1) * p1 + _sigmoid(g2 + b2) * p2


def _fgate_body(u_ref, wf_ref, bf_ref, cum_ref, carry_ref, *, ts):
    @pl.when(pl.program_id(1) == 0)
    def _():
        carry_ref[...] = jnp.zeros_like(carry_ref)

    f = jnp.dot(u_ref[...], wf_ref[...], preferred_element_type=F32) + bf_ref[...]
    lf = jnp.minimum(f, 0.0) - jnp.log1p(jnp.exp(-jnp.abs(f)))
    row = lax.broadcasted_iota(jnp.int32, (ts, ts), 0)
    col = lax.broadcasted_iota(jnp.int32, (ts, ts), 1)
    tri = (row >= col).astype(BF16)
    hi = lf.astype(BF16)
    r1 = lf - hi.astype(F32)
    mid = r1.astype(BF16)
    lo = (r1 - mid.astype(F32)).astype(BF16)
    cum = (jnp.dot(tri, hi, preferred_element_type=F32)
           + jnp.dot(tri, mid, preferred_element_type=F32)
           + jnp.dot(tri, lo, preferred_element_type=F32)) + carry_ref[0:1, :]
    cum_ref[...] = cum
    carry_ref[...] = jnp.broadcast_to(cum[ts - 1:ts, :], carry_ref.shape)


def _fgate(u2, wf, bf, batch, seq):
    ts = ROW_TILE
    nsb = seq // ts
    return pl.pallas_call(
        functools.partial(_fgate_body, ts=ts),
        grid=(batch, nsb),
        in_specs=[pl.BlockSpec((ts, D_MODEL), lambda b, s: (b * nsb + s, 0)),
                  pl.BlockSpec((D_MODEL, LANES), lambda b, s: (0, 0)),
                  pl.BlockSpec((1, LANES), lambda b, s: (0, 0))],
        out_specs=pl.BlockSpec((ts, LANES), lambda b, s: (b * nsb + s, 0)),
        out_shape=jax.ShapeDtypeStruct((batch * seq, LANES), F32),
        scratch_shapes=[pltpu.VMEM((SUBLANES, LANES), F32)],
        compiler_params=pltpu.CompilerParams(dimension_semantics=("arbitrary", "arbitrary")),
        name="fgate_cumsum",
    )(u2, wf, bf)


def _shifted(ext_ref, k, ts):
    return ext_ref[pl.ds(SUBLANES - k, ts), :]


def _lru_body(x_ref, g_ref, cw_ref, cb_ref, wa_ref, ba_ref, wx_ref, bx_ref, lam_ref, y_ref,
              ext_ref, h_ref, *, ts):
    @pl.when(pl.program_id(1) == 0)
    def _():
        ext_ref[0:SUBLANES, :] = jnp.zeros((SUBLANES, LRU_WIDTH), F32)
        h_ref[...] = jnp.zeros_like(h_ref)

    x = x_ref[...]
    ext_ref[pl.ds(SUBLANES, ts), :] = x
    cw = cw_ref[...]
    xc = x * cw[LRU_CONV - 1:LRU_CONV, :] + cb_ref[...]
    for k in range(1, LRU_CONV):
        xc = xc + _shifted(ext_ref, k, ts) * cw[LRU_CONV - 1 - k:LRU_CONV - k, :]
    ext_ref[0:SUBLANES, :] = x[ts - SUBLANES:ts, :]

    xcb = xc.astype(BF16)
    ra, ia = [], []
    for n in range(LRU_BLOCKS):
        blk = xcb[:, n * LRU_BLOCK:(n + 1) * LRU_BLOCK]
        ra.append(jnp.dot(blk, wa_ref[n].astype(BF16), preferred_element_type=F32))
        ia.append(jnp.dot(blk, wx_ref[n].astype(BF16), preferred_element_type=F32))
    r = _sigmoid(jnp.concatenate(ra, axis=1) + ba_ref[...])
    i = _sigmoid(jnp.concatenate(ia, axis=1) + bx_ref[...])
    log_a = (-LRU_C) * r * _softplus(-lam_ref[...])
    a = jnp.exp(log_a)
    th = jnp.tanh(log_a)
    b = jnp.sqrt(-2.0 * th / (1.0 - th)) * (i * xc)

    rowm = lax.broadcasted_iota(jnp.int32, (ts, LRU_WIDTH), 0) & (SUBLANES - 1)
    for d in (1, 2, 4):
        a_sh = pltpu.roll(a, d, 0)
        b_sh = pltpu.roll(b, d, 0)
        m = rowm >= d
        b = jnp.where(m, a * b_sh + b, b)
        a = jnp.where(m, a * a_sh, a)
    h = h_ref[0:1, :]
    outs = []
    for c in range(ts // SUBLANES):
        sl = slice(c * SUBLANES, (c + 1) * SUBLANES)
        hc = a[sl, :] * h + b[sl, :]
        outs.append(hc)
        h = hc[SUBLANES - 1:SUBLANES, :]
    h_ref[...] = jnp.broadcast_to(h, h_ref.shape)
    hs = jnp.concatenate(outs, axis=0)
    y_ref[...] = (hs * jax.nn.gelu(g_ref[...])).astype(y_ref.dtype)


def _lru_branch(z3, conv_w, conv_b, w_a, b_a, w_x, b_x, lam, l):
    batch, seq, _ = z3.shape
    ts = SEQ_TILE
    depth = conv_w.shape[0]
    vec = lambda p: p.reshape(depth, 1, LRU_WIDTH)
    vspec = pl.BlockSpec((None, 1, LRU_WIDTH), lambda b, s: (l, 0, 0))
    wspec = pl.BlockSpec((None, LRU_BLOCKS, LRU_BLOCK, LRU_BLOCK), lambda b, s: (l, 0, 0, 0))
    return pl.pallas_call(
        functools.partial(_lru_body, ts=ts),
        grid=(batch, seq // ts),
        in_specs=[pl.BlockSpec((None, ts, LRU_WIDTH), lambda b, s: (b, s, 0)),
                  pl.BlockSpec((None, ts, LRU_WIDTH), lambda b, s: (b, s, 1)),
                  pl.BlockSpec((None, LRU_CONV, LRU_WIDTH), lambda b, s: (l, 0, 0)),
                  vspec, wspec, vspec, wspec, vspec, vspec],
        out_specs=pl.BlockSpec((None, ts, LRU_WIDTH), lambda b, s: (b, s, 0)),
        out_shape=jax.ShapeDtypeStruct((batch, seq, LRU_WIDTH), BF16),
        scratch_shapes=[pltpu.VMEM((ts + SUBLANES, LRU_WIDTH), F32),
                        pltpu.VMEM((SUBLANES, LRU_WIDTH), F32)],
        compiler_params=pltpu.CompilerParams(dimension_semantics=("arbitrary", "arbitrary")),
        name="rglru_branch",
    )(z3, z3, conv_w, vec(conv_b), w_a, vec(b_a), w_x, vec(b_x), vec(lam))


def _sconv_body(b_ref, c_ref, h_ref, w_ref, y_ref, ext_ref, *, ts):
    @pl.when(pl.program_id(1) == 0)
    def _():
        ext_ref[0:SUBLANES, :] = jnp.zeros((SUBLANES, CONV_WIDTH), F32)

    ch = c_ref[...] * h_ref[...]
    ext_ref[pl.ds(SUBLANES, ts), :] = ch
    w = w_ref[...]
    y = ch * w[CONV_K - 1:CONV_K, :]
    for k in range(1, CONV_K):
        y = y + _shifted(ext_ref, k, ts) * w[CONV_K - 1 - k:CONV_K - k, :]
    ext_ref[0:SUBLANES, :] = ch[ts - SUBLANES:ts, :]
    y_ref[...] = (b_ref[...] * y).astype(y_ref.dtype)


def _sconv_branch(zt3, conv_w, l):
    batch, seq, _ = zt3.shape
    ts = SEQ_TILE
    col = lambda c: pl.BlockSpec((None, ts, CONV_WIDTH), lambda b, s: (b, s, c))
    return pl.pallas_call(
        functools.partial(_sconv_body, ts=ts),
        grid=(batch, seq // ts),
        in_specs=[col(0), col(1), col(2),
                  pl.BlockSpec((None, CONV_K, CONV_WIDTH), lambda b, s: (l, 0, 0))],
        out_specs=pl.BlockSpec((None, ts, CONV_WIDTH), lambda b, s: (b, s, 0)),
        out_shape=jax.ShapeDtypeStruct((batch, seq, CONV_WIDTH), BF16),
        scratch_shapes=[pltpu.VMEM((ts + SUBLANES, CONV_WIDTH), F32)],
        compiler_params=pltpu.CompilerParams(dimension_semantics=("arbitrary", "arbitrary")),
        name="short_conv_branch",
    )(zt3, zt3, zt3, conv_w)


def _head_norm(t, gain):
    return t * lax.rsqrt(jnp.mean(t * t, axis=-1, keepdims=True) + EPS) * gain


def _attn_body(q_ref, k_ref, v_ref, cc_ref, cr_ref, qg_ref, kg_ref, o_ref, *, seq, tq):
    scale = ATT_HEAD_DIM ** -0.5
    qn = _head_norm(q_ref[...], qg_ref[...]).astype(BF16)
    kn = _head_norm(k_ref[...], kg_ref[...]).astype(BF16)
    v = v_ref[...].astype(BF16)
    for qi in range(seq // tq):
        q0, kv = qi * tq, (qi + 1) * tq
        s = lax.dot_general(qn[q0:kv, :], kn[:kv, :], (((1,), (1,)), ((), ())),
                            preferred_element_type=F32) * scale
        s = s + (cc_ref[q0:kv, :] - cr_ref[:, :kv])
        row = lax.broadcasted_iota(jnp.int32, (tq, kv), 0) + q0
        col = lax.broadcasted_iota(jnp.int32, (tq, kv), 1)
        s = jnp.where(row >= col, s, -jnp.inf)
        p = jnp.exp(s - jnp.max(s, axis=-1, keepdims=True))
        denom = jnp.sum(p, axis=-1, keepdims=True)
        o = jnp.dot(p.astype(BF16), v[:kv, :], preferred_element_type=F32)
        o_ref[q0:kv, :] = (o / denom).astype(o_ref.dtype)


def _attention(z3, cum_col, cum_row, q_gain, k_gain, l):
    batch, seq, _ = z3.shape
    depth = q_gain.shape[0]
    hd = ATT_HEAD_DIM
    head = lambda base: pl.BlockSpec((None, seq, hd), lambda b, h: (b, 0, base + h))
    gspec = pl.BlockSpec((None, 1, hd), lambda b, h: (l, 0, 0))
    q_base = 2 * LRU_WIDTH // hd
    return pl.pallas_call(
        functools.partial(_attn_body, seq=seq, tq=Q_TILE),
        grid=(batch, ATT_HEADS),
        in_specs=[head(q_base), head(q_base + ATT_HEADS), head(q_base + 2 * ATT_HEADS),
                  pl.BlockSpec((None, None, seq, 1), lambda b, h: (b, h, 0, 0)),
                  pl.BlockSpec((None, None, 1, seq), lambda b, h: (b, h, 0, 0)),
                  gspec, gspec],
        out_specs=pl.BlockSpec((None, seq, hd), lambda b, h: (b, 0, h)),
        out_shape=jax.ShapeDtypeStruct((batch, seq, ATT_WIDTH), BF16),
        compiler_params=pltpu.CompilerParams(
            dimension_semantics=("arbitrary", "arbitrary"),
            vmem_limit_bytes=_vmem_limit([3 * _nbytes((seq, hd), F32), _nbytes((seq, LANES), F32),
                                          _nbytes((SUBLANES, seq), F32), _nbytes((seq, hd), BF16)],
                                         8 * _nbytes((Q_TILE, seq), F32))),
        name="forgetting_attention",
    )(z3, z3, z3, cum_col, cum_row, q_gain.reshape(depth, 1, hd), k_gain.reshape(depth, 1, hd))


def _router_body(x_ref, g_ref, rw_ref, rb_ref, ri_ref, rwt_ref, cnt_ref, carry_ref, *, bm):
    @pl.when(pl.program_id(0) == 0)
    def _():
        carry_ref[...] = jnp.zeros_like(carry_ref)

    x = x_ref[...]
    h = x * lax.rsqrt(jnp.mean(x * x, axis=-1, keepdims=True) + EPS) * g_ref[...]
    logits = jnp.dot(h, rw_ref[...], preferred_element_type=F32,
                     precision=lax.Precision.HIGHEST) + rb_ref[...]
    lane = lax.broadcasted_iota(jnp.int32, (bm, LANES), 1)
    logits = jnp.where(lane < N_EXPERTS, logits, -jnp.inf)
    m1 = jnp.max(logits, axis=-1, keepdims=True)
    i1 = jnp.min(jnp.where(logits == m1, lane, LANES), axis=-1, keepdims=True)
    rest = jnp.where(lane == i1, -jnp.inf, logits)
    m2 = jnp.max(rest, axis=-1, keepdims=True)
    i2 = jnp.min(jnp.where(rest == m2, lane, LANES), axis=-1, keepdims=True)
    e2 = jnp.exp(m2 - m1)
    w1 = 1.0 / (1.0 + e2)
    w2 = e2 / (1.0 + e2)

    onehot = ((lane == i1) | (lane == i2)).astype(BF16)
    row = lax.broadcasted_iota(jnp.int32, (bm, bm), 0)
    col = lax.broadcasted_iota(jnp.int32, (bm, bm), 1)
    tri = (row > col).astype(BF16)
    before = jnp.dot(tri, onehot, preferred_element_type=F32) + carry_ref[0:1, :]
    rank1 = jnp.sum(jnp.where(lane == i1, before, 0.0), axis=-1, keepdims=True).astype(jnp.int32)
    rank2 = jnp.sum(jnp.where(lane == i2, before, 0.0), axis=-1, keepdims=True).astype(jnp.int32)
    total = carry_ref[0:1, :] + jnp.sum(onehot.astype(F32), axis=0, keepdims=True)
    carry_ref[...] = jnp.broadcast_to(total, carry_ref.shape)
    cnt_ref[...] = jnp.broadcast_to(total, cnt_ref.shape)

    ri_ref[...] = jnp.where(lane == 0, i1, jnp.where(lane == 1, i2,
                            jnp.where(lane == 2, rank1, jnp.where(lane == 3, rank2, 0))))
    rwt_ref[...] = jnp.where(lane == 0, w1, jnp.where(lane == 1, w2, 0.0))


def _router(x2, gain, rw, rb, l):
    n, d = x2.shape
    bm = ROW_TILE
    return pl.pallas_call(
        functools.partial(_router_body, bm=bm),
        grid=(n // bm,),
        in_specs=[pl.BlockSpec((bm, d), lambda i: (i, 0)),
                  pl.BlockSpec((None, 1, d), lambda i: (l, 0, 0)),
                  pl.BlockSpec((d, LANES), lambda i: (0, 0)),
                  pl.BlockSpec((1, LANES), lambda i: (0, 0))],
        out_specs=[pl.BlockSpec((bm, LANES), lambda i: (i, 0)),
                   pl.BlockSpec((bm, LANES), lambda i: (i, 0)),
                   pl.BlockSpec((SUBLANES, LANES), lambda i: (0, 0))],
        out_shape=[jax.ShapeDtypeStruct((n, LANES), jnp.int32),
                   jax.ShapeDtypeStruct((n, LANES), F32),
                   jax.ShapeDtypeStruct((SUBLANES, LANES), F32)],
        scratch_shapes=[pltpu.VMEM((SUBLANES, LANES), F32)],
        compiler_params=pltpu.CompilerParams(dimension_semantics=("arbitrary",)),
        name="moe_router",
    )(x2, gain.reshape(gain.shape[0], 1, d), rw, rb)


def _row_copy(src_hbm, dst_vmem, sem, src_row, dst_row):
    return pltpu.make_async_copy(src_hbm.at[pl.ds(src_row, 1), :], dst_vmem.at[pl.ds(dst_row, 1), :], sem)


def _dispatch_body(tok_ref, x_hbm, g_ref, o_ref, buf_ref, sem, *, tg):
    base = pl.program_id(0) * tg

    def issue(r, c):
        _row_copy(x_hbm, buf_ref, sem, tok_ref[base + r], r).start()
        return c

    def drain(r, c):
        _row_copy(x_hbm, buf_ref, sem, 0, r).wait()
        return c

    lax.fori_loop(0, tg, issue, 0)
    lax.fori_loop(0, tg, drain, 0)
    x = buf_ref[...]
    h = x * lax.rsqrt(jnp.mean(x * x, axis=-1, keepdims=True) + EPS) * g_ref[...]
    o_ref[...] = h.astype(o_ref.dtype)


def _dispatch(row_token, x2, gain, l, rows):
    n, d = x2.shape
    tg = GATHER_TILE
    return pl.pallas_call(
        functools.partial(_dispatch_body, tg=tg),
        grid_spec=pltpu.PrefetchScalarGridSpec(
            num_scalar_prefetch=1,
            grid=(rows // tg,),
            in_specs=[pl.BlockSpec(memory_space=pl.ANY),
                      pl.BlockSpec((None, 1, d), lambda i, tok: (l, 0, 0))],
            out_specs=pl.BlockSpec((tg, d), lambda i, tok: (i, 0)),
            scratch_shapes=[pltpu.VMEM((tg, d), F32), pltpu.SemaphoreType.DMA(())]),
        out_shape=jax.ShapeDtypeStruct((rows, d), BF16),
        compiler_params=pltpu.CompilerParams(dimension_semantics=("arbitrary",)),
        name="moe_dispatch_gather",
    )(row_token, x2, gain.reshape(gain.shape[0], 1, d))


def _moe_up_body(te_ref, nu_ref, a_ref, wg_ref, wu_ref, o_ref):
    used = pl.program_id(1) < nu_ref[0]

    @pl.when(used)
    def _():
        a = a_ref[...]
        g = jnp.dot(a, wg_ref[...].astype(BF16), preferred_element_type=F32)
        u = jnp.dot(a, wu_ref[...].astype(BF16), preferred_element_type=F32)
        o_ref[...] = ((g * _sigmoid(g)) * u).astype(o_ref.dtype)

    @pl.when(jnp.logical_not(used))
    def _():
        o_ref[...] = jnp.zeros_like(o_ref)


def _moe_down_body(te_ref, nu_ref, a_ref, wd_ref, o_ref):
    used = pl.program_id(1) < nu_ref[0]

    @pl.when(used)
    def _():
        o_ref[...] = jnp.dot(a_ref[...], wd_ref[...].astype(BF16), preferred_element_type=F32)

    @pl.when(jnp.logical_not(used))
    def _():
        o_ref[...] = jnp.zeros_like(o_ref)


def _tile_clamp(t, nu):
    return jnp.minimum(t, nu[0] - 1)


def _moe_up(tile_expert, n_used, xs, wg, wu, m):
    rows, d = xs.shape
    dff = wg.shape[-1]
    tm, bn = MOE_TILE, 1024
    wspec = pl.BlockSpec((None, None, d, bn), lambda j, t, te, nu: (m, te[_tile_clamp(t, nu)], 0, j))
    blocks = [_nbytes((tm, d), BF16), 2 * _nbytes((d, bn), F32), _nbytes((tm, bn), BF16)]
    temps = 2 * _nbytes((d, bn), BF16) + 3 * _nbytes((tm, bn), F32)
    return pl.pallas_call(
        _moe_up_body,
        grid_spec=pltpu.PrefetchScalarGridSpec(
            num_scalar_prefetch=2,
            grid=(dff // bn, rows // tm),
            in_specs=[pl.BlockSpec((tm, d), lambda j, t, te, nu: (_tile_clamp(t, nu), 0)), wspec, wspec],
            out_specs=pl.BlockSpec((tm, bn), lambda j, t, te, nu: (t, j))),
        out_shape=jax.ShapeDtypeStruct((rows, dff), BF16),
        compiler_params=pltpu.CompilerParams(dimension_semantics=("arbitrary", "arbitrary"),
                                             vmem_limit_bytes=_vmem_limit(blocks, temps)),
        name="moe_up_swiglu",
    )(tile_expert, n_used, xs, wg, wu)


def _moe_down(tile_expert, n_used, act, wd, m):
    rows, dff = act.shape
    d = wd.shape[-1]
    tm, bn = MOE_TILE, 512
    blocks = [_nbytes((tm, dff), BF16), _nbytes((dff, bn), F32), _nbytes((tm, bn), F32)]
    temps = _nbytes((dff, bn), BF16) + 2 * _nbytes((tm, bn), F32)
    return pl.pallas_call(
        _moe_down_body,
        grid_spec=pltpu.PrefetchScalarGridSpec(
            num_scalar_prefetch=2,
            grid=(d // bn, rows // tm),
            in_specs=[pl.BlockSpec((tm, dff), lambda j, t, te, nu: (_tile_clamp(t, nu), 0)),
                      pl.BlockSpec((None, None, dff, bn),
                                   lambda j, t, te, nu: (m, te[_tile_clamp(t, nu)], 0, j))],
            out_specs=pl.BlockSpec((tm, bn), lambda j, t, te, nu: (t, j))),
        out_shape=jax.ShapeDtypeStruct((rows, d), F32),
        compiler_params=pltpu.CompilerParams(dimension_semantics=("arbitrary", "arbitrary"),
                                             vmem_limit_bytes=_vmem_limit(blocks, temps)),
        name="moe_down",
    )(tile_expert, n_used, act, wd)


def _combine_body(pos_ref, y_hbm, x_ref, w_ref, o_ref, buf_ref, sem, *, tc):
    base = pl.program_id(0) * tc

    def issue(r, c):
        for k in range(TOP_K):
            _row_copy(y_hbm, buf_ref.at[k], sem, pos_ref[(base + r) * TOP_K + k], r).start()
        return c

    def drain(r, c):
        for k in range(TOP_K):
            _row_copy(y_hbm, buf_ref.at[k], sem, 0, r).wait()
        return c

    lax.fori_loop(0, tc, issue, 0)
    lax.fori_loop(0, tc, drain, 0)
    w = w_ref[...]
    o_ref[...] = x_ref[...] + w[:, 0:1] * buf_ref[0] + w[:, 1:2] * buf_ref[1]


def _combine(pos_flat, y, x2, route_w):
    n, d = x2.shape
    tc = GATHER_TILE
    return pl.pallas_call(
        functools.partial(_combine_body, tc=tc),
        grid_spec=pltpu.PrefetchScalarGridSpec(
            num_scalar_prefetch=1,
            grid=(n // tc,),
            in_specs=[pl.BlockSpec(memory_space=pl.ANY),
                      pl.BlockSpec((tc, d), lambda i, pos: (i, 0)),
                      pl.BlockSpec((tc, LANES), lambda i, pos: (i, 0))],
            out_specs=pl.BlockSpec((tc, d), lambda i, pos: (i, 0)),
            scratch_shapes=[pltpu.VMEM((TOP_K, tc, d), F32), pltpu.SemaphoreType.DMA(())]),
        out_shape=jax.ShapeDtypeStruct((n, d), F32),
        compiler_params=pltpu.CompilerParams(dimension_semantics=("arbitrary",)),
        name="moe_combine_gather",
    )(pos_flat, y, x2, route_w)


def _routing_tables(route_i, counts, n_tokens, rows):
    tm = MOE_TILE
    expert = route_i[:, 0:TOP_K]
    rank = route_i[:, TOP_K:2 * TOP_K]
    cnt = counts[0, :N_EXPERTS].astype(jnp.int32)
    tiles_e = (cnt + (tm - 1)) // tm
    tile_end = jnp.cumsum(tiles_e)
    row_off = (tile_end - tiles_e) * tm
    pos = row_off[expert] + rank
    token = jnp.broadcast_to(jnp.arange(n_tokens, dtype=jnp.int32)[:, None], pos.shape)
    row_token = jnp.zeros((rows,), jnp.int32).at[pos.reshape(-1)].set(token.reshape(-1))
    tiles = jnp.arange(rows // tm, dtype=jnp.int32)
    tile_expert = jnp.minimum(jnp.sum(tiles[:, None] >= tile_end[None, :], axis=1),
                              N_EXPERTS - 1).astype(jnp.int32)
    n_used = tile_end[N_EXPERTS - 1:N_EXPERTS].astype(jnp.int32)
    return pos.reshape(-1).astype(jnp.int32), row_token, tile_expert, n_used


def _moe_ffn(x2, gain, l, router_w, router_b, wg, wu, wd, m):
    n, d = x2.shape
    rows = n * TOP_K + N_EXPERTS * MOE_TILE
    rw = jnp.pad(router_w[m], ((0, 0), (0, LANES - N_EXPERTS)))
    rb = jnp.pad(router_b[m], (0, LANES - N_EXPERTS)).reshape(1, LANES)
    route_i, route_w, counts = _router(x2, gain, rw, rb, l)
    pos_flat, row_token, tile_expert, n_used = _routing_tables(route_i, counts, n, rows)
    xs = _dispatch(row_token, x2, gain, l, rows)
    act = _moe_up(tile_expert, n_used, xs, wg, wu, m)
    y = _moe_down(tile_expert, n_used, act, wd, m)
    return _combine(pos_flat, y, x2, route_w)


def _dense_ffn(x2, gain, l, wg, wu, wd, m):
    n, d = x2.shape
    dff = wg.shape[-1]
    h = _rmsnorm(x2, gain, l)
    bm, bn = 1024, 512
    act = _mm("ffn_up_swiglu", n, dff, bm, bn,
              [(h, _a_spec(bm, d))],
              [(wg, _w_spec(d, bn, m)), (wu, _w_spec(d, bn, m))], [],
              [(0, 0), (0, 1)], _ep_swiglu, BF16)
    bm, bn = 512, 512
    return _mm("ffn_down_residual", n, d, bm, bn,
               [(act, _a_spec(bm, dff))],
               [(wd, _w_spec(dff, bn, m))],
               [(x2, pl.BlockSpec((bm, bn), lambda j, i: (i, j)))],
               [(0, 0)], _ep_residual, F32)


def _mixer(x2, l, batch, seq, norm_mix_g, w_in, lru_conv_w, lru_conv_b, lru_wa, lru_ba, lru_wx, lru_bx,
           lru_lambda, fox_bf, q_norm_g, k_norm_g, sc_conv_w, w_branch_lru, w_branch_att, w_branch_conv,
           w_merge, b_merge, w_out):
    n, d = x2.shape
    depth = w_in.shape[0]
    u = _rmsnorm(x2, norm_mix_g, l)

    bm, bn = 1024, 1024
    z = _mm("in_proj_main", n, COL_MAIN, bm, bn, [(u, _a_spec(bm, d))], [(w_in, _w_spec(d, bn, l))], [],
            [(0, 0)], _ep_identity, F32)
    w_tail = w_in[l][:, COL_TAIL:].astype(BF16)[None]
    zt = _mm("in_proj_tail", n, 3 * CONV_WIDTH, bm, bn, [(u, _a_spec(bm, d))],
             [(w_tail, _w_spec(d, bn, 0))], [], [(0, 0)], _ep_identity, F32)
    z3 = z.reshape(batch, seq, COL_MAIN)
    zt3 = zt.reshape(batch, seq, 3 * CONV_WIDTH)

    wf = jnp.pad(w_in[l][:, COL_F:COL_TAIL], ((0, 0), (0, LANES - ATT_HEADS))).astype(BF16)
    bf = jnp.pad(fox_bf[l], (0, LANES - ATT_HEADS)).reshape(1, LANES)
    cum = _fgate(u, wf, bf, batch, seq)[:, :ATT_HEADS].reshape(batch, seq, ATT_HEADS)
    cum_h = cum.transpose(0, 2, 1)
    cum_col = cum_h.reshape(batch, ATT_HEADS, seq, 1)
    cum_row = cum_h.reshape(batch, ATT_HEADS, 1, seq)

    y_lru = _lru_branch(z3, lru_conv_w, lru_conv_b, lru_wa, lru_ba, lru_wx, lru_bx, lru_lambda, l)
    y_att = _attention(z3, cum_col, cum_row, q_norm_g, k_norm_g, l)
    y_conv = _sconv_branch(zt3, sc_conv_w, l)

    bm, bn = 1024, 256
    nb = d // bn
    merged = _mm(
        "gated_merge", n, d, bm, bn,
        [(u, _a_spec(bm, d)), (y_lru.reshape(n, LRU_WIDTH), _a_spec(bm, LRU_WIDTH)),
         (y_att.reshape(n, ATT_WIDTH), _a_spec(bm, ATT_WIDTH)),
         (y_conv.reshape(n, CONV_WIDTH), _a_spec(bm, CONV_WIDTH))],
        [(w_merge, _w_spec(d, bn, l, 0)), (w_merge, _w_spec(d, bn, l, nb)), (w_merge, _w_spec(d, bn, l, 2 * nb)),
         (w_branch_lru, _w_spec(LRU_WIDTH, bn, l)), (w_branch_att, _w_spec(ATT_WIDTH, bn, l)),
         (w_branch_conv, _w_spec(CONV_WIDTH, bn, l))],
        [(b_merge.reshape(depth, 1, -1), _row_spec(bn, l, g * nb)) for g in range(3)],
        [(0, 0), (0, 1), (0, 2), (1, 3), (2, 4), (3, 5)], _ep_merge, BF16)

    bm, bn = 1024, 512
    return _mm("out_proj_residual", n, d, bm, bn, [(merged, _a_spec(bm, d))], [(w_out, _w_spec(d, bn, l))],
               [(x2, pl.BlockSpec((bm, bn), lambda j, i: (i, j)))], [(0, 0)], _ep_residual, F32)


def kernel(x, norm_mix_g, w_in, lru_conv_w, lru_conv_b, lru_wa, lru_ba, lru_wx, lru_bx, lru_lambda, fox_bf, q_norm_g, k_norm_g, sc_conv_w, w_branch_lru, w_branch_att, w_branch_conv, w_merge, b_merge, w_out, norm_ffn_g, ffn_wg, ffn_wu, ffn_wd, router_w, router_b, moe_wg, moe_wu, moe_wd):
    batch, seq, d = x.shape
    depth = w_in.shape[0]
    x2 = x.reshape(batch * seq, d)
    for l in range(depth):
        x2 = _mixer(x2, l, batch, seq, norm_mix_g, w_in, lru_conv_w, lru_conv_b, lru_wa, lru_ba, lru_wx,
                    lru_bx, lru_lambda, fox_bf, q_norm_g, k_norm_g, sc_conv_w, w_branch_lru, w_branch_att,
                    w_branch_conv, w_merge, b_merge, w_out)
        if l % 2 == 0:
            x2 = _dense_ffn(x2, norm_ffn_g, l, ffn_wg, ffn_wu, ffn_wd, l // 2)
        else:
            x2 = _moe_ffn(x2, norm_ffn_g, l, router_w, router_b, moe_wg, moe_wu, moe_wd, l // 2)
    return x2.reshape(batch, seq, d)
```

```python
import functools

import jax
import jax.numpy as jnp
from jax import lax
from jax.experimental import pallas as pl
from jax.experimental.pallas import tpu as pltpu

F32 = jnp.float32
BF16 = jnp.bfloat16

D_MODEL = 2048
LRU_WIDTH = 1024
LRU_BLOCKS = 8
LRU_BLOCK = 128
LRU_CONV = 4
LRU_C = 8.0
ATT_HEADS = 8
ATT_HEAD_DIM = 128
ATT_WIDTH = 1024
CONV_WIDTH = 1024
CONV_K = 3
N_EXPERTS = 8
TOP_K = 2
EPS = 1e-6
COL_MAIN = 2 * LRU_WIDTH + 3 * ATT_WIDTH
COL_F = COL_MAIN
COL_TAIL = COL_MAIN + ATT_HEADS

LANES = 128
SUBLANES = 8
V7X_VMEM_BUDGET = 56 * 1024 * 1024

ROW_TILE = 512
SEQ_TILE = 256
Q_TILE = 256
MOE_TILE = 512
GATHER_TILE = 256
GATHER_UNROLL = 8


def _vmem_limit(block_bytes, temp_bytes=0, single_bytes=0):
    need = 2 * sum(block_bytes) + single_bytes + temp_bytes + (4 << 20)
    return int(min(max(need, 16 << 20), V7X_VMEM_BUDGET))


def _nbytes(shape, dtype):
    n = 1
    for s in shape:
        n *= s
    return n * jnp.dtype(dtype).itemsize


def _softplus(y):
    return jnp.maximum(y, 0.0) + jnp.log1p(jnp.exp(-jnp.abs(y)))


def _sigmoid(y):
    return 1.0 / (1.0 + jnp.exp(-y))


def _rmsnorm_body(x_ref, g_ref, o_ref):
    x = x_ref[...]
    y = x * lax.rsqrt(jnp.mean(x * x, axis=-1, keepdims=True) + EPS)
    o_ref[...] = (y * g_ref[...]).astype(o_ref.dtype)


def _rmsnorm(x2, gain, l):
    n, d = x2.shape
    return pl.pallas_call(
        _rmsnorm_body,
        grid=(n // ROW_TILE,),
        in_specs=[pl.BlockSpec((ROW_TILE, d), lambda i: (i, 0)),
                  pl.BlockSpec((None, 1, d), lambda i: (l, 0, 0))],
        out_specs=pl.BlockSpec((ROW_TILE, d), lambda i: (i, 0)),
        out_shape=jax.ShapeDtypeStruct((n, d), BF16),
        compiler_params=pltpu.CompilerParams(dimension_semantics=("arbitrary",)),
        name="rmsnorm",
    )(x2, gain.reshape(gain.shape[0], 1, d))


def _mm_body(*refs, pairs, n_a, n_w, n_e, epilogue):
    a_refs = refs[:n_a]
    w_refs = refs[n_a:n_a + n_w]
    e_refs = refs[n_a + n_w:n_a + n_w + n_e]
    o_ref = refs[-1]
    accs = [jnp.dot(a_refs[ai][...], w_refs[wi][...].astype(BF16), preferred_element_type=F32)
            for ai, wi in pairs]
    o_ref[...] = epilogue(accs, [e[...] for e in e_refs]).astype(o_ref.dtype)


def _mm(name, m, n, bm, bn, a_ops, w_ops, e_ops, pairs, epilogue, out_dtype):
    arrays = [a for a, _ in a_ops] + [w for w, _ in w_ops] + [e for e, _ in e_ops]
    specs = [s for _, s in a_ops] + [s for _, s in w_ops] + [s for _, s in e_ops]
    blocks, single = [], 0
    for arr, spec in a_ops + w_ops + e_ops:
        nb = _nbytes([s for s in spec.block_shape if s is not None], arr.dtype)
        if spec.pipeline_mode is not None and spec.pipeline_mode.buffer_count == 1:
            single += nb
        else:
            blocks.append(nb)
    blocks.append(_nbytes((bm, bn), out_dtype))
    w_bf16 = sum(_nbytes([s for s in spec.block_shape if s is not None], BF16)
                 for w, spec in w_ops if w.dtype != BF16)
    temps = w_bf16 + len(pairs) * _nbytes((bm, bn), F32)
    body = functools.partial(_mm_body, pairs=tuple(pairs), n_a=len(a_ops), n_w=len(w_ops),
                             n_e=len(e_ops), epilogue=epilogue)
    return pl.pallas_call(
        body,
        grid=(pl.cdiv(n, bn), m // bm),
        in_specs=specs,
        out_specs=pl.BlockSpec((bm, bn), lambda j, i: (i, j)),
        out_shape=jax.ShapeDtypeStruct((m, n), out_dtype),
        compiler_params=pltpu.CompilerParams(
            dimension_semantics=("arbitrary", "arbitrary"),
            vmem_limit_bytes=_vmem_limit(blocks, temps, single)),
        name=name,
    )(*arrays)


def _a_spec(bm, k):
    return pl.BlockSpec((bm, k), lambda j, i: (i, 0))


def _w_spec(k, bn, l, col_block0=0, single_buffer=False):
    mode = pl.Buffered(1) if single_buffer else None
    return pl.BlockSpec((None, k, bn), lambda j, i: (l, 0, j + col_block0), pipeline_mode=mode)


def _row_spec(bn, l, col_block0=0):
    return pl.BlockSpec((None, 1, bn), lambda j, i: (l, 0, j + col_block0))


def _ep_identity(accs, extras):
    return accs[0]


def _ep_residual(accs, extras):
    return extras[0] + accs[0]


def _ep_swiglu(accs, extras):
    g, u = accs
    return (g * _sigmoid(g)) * u


def _ep_merge(accs, extras):
    g0, g1, g2, p0, p1, p2 = accs
    b0, b1, b2 = extras
    return _sigmoid(g0 + b0) * p0 + _sigmoid(g1 + b1) * p1 + _sigmoid(g2 + b2) * p2


def _fgate_body(u_ref, wf_ref, bf_ref, cum_ref, carry_ref, *, ts):
    @pl.when(pl.program_id(1) == 0)
    def _():
        carry_ref[...] = jnp.zeros_like(carry_ref)

    f = jnp.dot(u_ref[...], wf_ref[...], preferred_element_type=F32) + bf_ref[...]
    lf = jnp.minimum(f, 0.0) - jnp.log1p(jnp.exp(-jnp.abs(f)))
    row = lax.broadcasted_iota(jnp.int32, (ts, ts), 0)
    col = lax.broadcasted_iota(jnp.int32, (ts, ts), 1)
    tri = (row >= col).astype(BF16)
    hi = lf.astype(BF16)
    r1 = lf - hi.astype(F32)
    mid = r1.astype(BF16)
    lo = (r1 - mid.astype(F32)).astype(BF16)
    cum = (jnp.dot(tri, hi, preferred_element_type=F32)
           + jnp.dot(tri, mid, preferred_element_type=F32)
           + jnp.dot(tri, lo, preferred_element_type=F32)) + carry_ref[0:1, :]
    cum_ref[...] = cum
    carry_ref[...] = jnp.broadcast_to(cum[ts - 1:ts, :], carry_ref.shape)


def _fgate(u2, wf, bf, batch, seq):
    ts = ROW_TILE
    nsb = seq // ts
    return pl.pallas_call(
        functools.partial(_fgate_body, ts=ts),
        grid=(batch, nsb),
        in_specs=[pl.BlockSpec((ts, D_MODEL), lambda b, s: (b * nsb + s, 0)),
                  pl.BlockSpec((D_MODEL, LANES), lambda b, s: (0, 0)),
                  pl.BlockSpec((1, LANES), lambda b, s: (0, 0))],
        out_specs=pl.BlockSpec((ts, LANES), lambda b, s: (b * nsb + s, 0)),
        out_shape=jax.ShapeDtypeStruct((batch * seq, LANES), F32),
        scratch_shapes=[pltpu.VMEM((SUBLANES, LANES), F32)],
        compiler_params=pltpu.CompilerParams(dimension_semantics=("arbitrary", "arbitrary")),
        name="fgate_cumsum",
    )(u2, wf, bf)


def _shifted(ext_ref, k, ts):
    return ext_ref[pl.ds(SUBLANES - k, ts), :]


def _lru_body(x_ref, g_ref, cw_ref, cb_ref, wa_ref, ba_ref, wx_ref, bx_ref, lam_ref, y_ref,
              ext_ref, h_ref, *, ts):
    @pl.when(pl.program_id(1) == 0)
    def _():
        ext_ref[0:SUBLANES, :] = jnp.zeros((SUBLANES, LRU_WIDTH), F32)
        h_ref[...] = jnp.zeros_like(h_ref)

    x = x_ref[...]
    ext_ref[pl.ds(SUBLANES, ts), :] = x
    cw = cw_ref[...]
    xc = x * cw[LRU_CONV - 1:LRU_CONV, :] + cb_ref[...]
    for k in range(1, LRU_CONV):
        xc = xc + _shifted(ext_ref, k, ts) * cw[LRU_CONV - 1 - k:LRU_CONV - k, :]
    ext_ref[0:SUBLANES, :] = x[ts - SUBLANES:ts, :]

    xcb = xc.astype(BF16)
    ra, ia = [], []
    for n in range(LRU_BLOCKS):
        blk = xcb[:, n * LRU_BLOCK:(n + 1) * LRU_BLOCK]
        ra.append(jnp.dot(blk, wa_ref[n].astype(BF16), preferred_element_type=F32))
        ia.append(jnp.dot(blk, wx_ref[n].astype(BF16), preferred_element_type=F32))
    r = _sigmoid(jnp.concatenate(ra, axis=1) + ba_ref[...])
    i = _sigmoid(jnp.concatenate(ia, axis=1) + bx_ref[...])
    log_a = (-LRU_C) * r * _softplus(-lam_ref[...])
    a = jnp.exp(log_a)
    th = jnp.tanh(log_a)
    b = jnp.sqrt(-2.0 * th / (1.0 - th)) * (i * xc)

    rowm = lax.broadcasted_iota(jnp.int32, (ts, LRU_WIDTH), 0) & (SUBLANES - 1)
    for d in (1, 2, 4):
        a_sh = pltpu.roll(a, d, 0)
        b_sh = pltpu.roll(b, d, 0)
        m = rowm >= d
        b = jnp.where(m, a * b_sh + b, b)
        a = jnp.where(m, a * a_sh, a)
    h = h_ref[0:1, :]
    outs = []
    for c in range(ts // SUBLANES):
        sl = slice(c * SUBLANES, (c + 1) * SUBLANES)
        hc = a[sl, :] * h + b[sl, :]
        outs.append(hc)
        h = hc[SUBLANES - 1:SUBLANES, :]
    h_ref[...] = jnp.broadcast_to(h, h_ref.shape)
    hs = jnp.concatenate(outs, axis=0)
    y_ref[...] = (hs * jax.nn.gelu(g_ref[...])).astype(y_ref.dtype)


def _lru_branch(z3, conv_w, conv_b, w_a, b_a, w_x, b_x, lam, l):
    batch, seq, _ = z3.shape
    ts = SEQ_TILE
    depth = conv_w.shape[0]
    vec = lambda p: p.reshape(depth, 1, LRU_WIDTH)
    vspec = pl.BlockSpec((None, 1, LRU_WIDTH), lambda b, s: (l, 0, 0))
    wspec = pl.BlockSpec((None, LRU_BLOCKS, LRU_BLOCK, LRU_BLOCK), lambda b, s: (l, 0, 0, 0))
    return pl.pallas_call(
        functools.partial(_lru_body, ts=ts),
        grid=(batch, seq // ts),
        in_specs=[pl.BlockSpec((None, ts, LRU_WIDTH), lambda b, s: (b, s, 0)),
                  pl.BlockSpec((None, ts, LRU_WIDTH), lambda b, s: (b, s, 1)),
                  pl.BlockSpec((None, LRU_CONV, LRU_WIDTH), lambda b, s: (l, 0, 0)),
                  vspec, wspec, vspec, wspec, vspec, vspec],
        out_specs=pl.BlockSpec((None, ts, LRU_WIDTH), lambda b, s: (b, s, 0)),
        out_shape=jax.ShapeDtypeStruct((batch, seq, LRU_WIDTH), BF16),
        scratch_shapes=[pltpu.VMEM((ts + SUBLANES, LRU_WIDTH), F32),
                        pltpu.VMEM((SUBLANES, LRU_WIDTH), F32)],
        compiler_params=pltpu.CompilerParams(dimension_semantics=("arbitrary", "arbitrary")),
        name="rglru_branch",
    )(z3, z3, conv_w, vec(conv_b), w_a, vec(b_a), w_x, vec(b_x), vec(lam))


def _sconv_body(b_ref, c_ref, h_ref, w_ref, y_ref, ext_ref, *, ts):
    @pl.when(pl.program_id(1) == 0)
    def _():
        ext_ref[0:SUBLANES, :] = jnp.zeros((SUBLANES, CONV_WIDTH), F32)

    ch = c_ref[...] * h_ref[...]
    ext_ref[pl.ds(SUBLANES, ts), :] = ch
    w = w_ref[...]
    y = ch * w[CONV_K - 1:CONV_K, :]
    for k in range(1, CONV_K):
        y = y + _shifted(ext_ref, k, ts) * w[CONV_K - 1 - k:CONV_K - k, :]
    ext_ref[0:SUBLANES, :] = ch[ts - SUBLANES:ts, :]
    y_ref[...] = (b_ref[...] * y).astype(y_ref.dtype)


def _sconv_branch(zt3, conv_w, l):
    batch, seq, _ = zt3.shape
    ts = SEQ_TILE
    col = lambda c: pl.BlockSpec((None, ts, CONV_WIDTH), lambda b, s: (b, s, c))
    return pl.pallas_call(
        functools.partial(_sconv_body, ts=ts),
        grid=(batch, seq // ts),
        in_specs=[col(0), col(1), col(2),
                  pl.BlockSpec((None, CONV_K, CONV_WIDTH), lambda b, s: (l, 0, 0))],
        out_specs=pl.BlockSpec((None, ts, CONV_WIDTH), lambda b, s: (b, s, 0)),
        out_shape=jax.ShapeDtypeStruct((batch, seq, CONV_WIDTH), BF16),
        scratch_shapes=[pltpu.VMEM((ts + SUBLANES, CONV_WIDTH), F32)],
        compiler_params=pltpu.CompilerParams(dimension_semantics=("arbitrary", "arbitrary")),
        name="short_conv_branch",
    )(zt3, zt3, zt3, conv_w)


def _head_norm(t, gain):
    return t * lax.rsqrt(jnp.mean(t * t, axis=-1, keepdims=True) + EPS) * gain


def _attn_body(q_ref, k_ref, v_ref, cc_ref, cr_ref, qg_ref, kg_ref, o_ref, *, seq, tq):
    scale = ATT_HEAD_DIM ** -0.5
    qn = _head_norm(q_ref[...], qg_ref[...]).astype(BF16)
    kn = _head_norm(k_ref[...], kg_ref[...]).astype(BF16)
    v = v_ref[...].astype(BF16)
    for qi in range(seq // tq):
        q0, kv = qi * tq, (qi + 1) * tq
        s = lax.dot_general(qn[q0:kv, :], kn[:kv, :], (((1,), (1,)), ((), ())),
                            preferred_element_type=F32) * scale
        s = s + (cc_ref[q0:kv, :] - cr_ref[:, :kv])
        row = lax.broadcasted_iota(jnp.int32, (tq, kv), 0) + q0
        col = lax.broadcasted_iota(jnp.int32, (tq, kv), 1)
        s = jnp.where(row >= col, s, -jnp.inf)
        p = jnp.exp(s - jnp.max(s, axis=-1, keepdims=True))
        denom = jnp.sum(p, axis=-1, keepdims=True)
        o = jnp.dot(p.astype(BF16), v[:kv, :], preferred_element_type=F32)
        o_ref[q0:kv, :] = (o / denom).astype(o_ref.dtype)


def _attention(z3, cum_col, cum_row, q_gain, k_gain, l):
    batch, seq, _ = z3.shape
    depth = q_gain.shape[0]
    hd = ATT_HEAD_DIM
    head = lambda base: pl.BlockSpec((None, seq, hd), lambda b, h: (b, 0, base + h))
    gspec = pl.BlockSpec((None, 1, hd), lambda b, h: (l, 0, 0))
    q_base = 2 * LRU_WIDTH // hd
    return pl.pallas_call(
        functools.partial(_attn_body, seq=seq, tq=Q_TILE),
        grid=(batch, ATT_HEADS),
        in_specs=[head(q_base), head(q_base + ATT_HEADS), head(q_base + 2 * ATT_HEADS),
                  pl.BlockSpec((None, None, seq, 1), lambda b, h: (b, h, 0, 0)),
                  pl.BlockSpec((None, None, 1, seq), lambda b, h: (b, h, 0, 0)),
                  gspec, gspec],
        out_specs=pl.BlockSpec((None, seq, hd), lambda b, h: (b, 0, h)),
        out_shape=jax.ShapeDtypeStruct((batch, seq, ATT_WIDTH), BF16),
        compiler_params=pltpu.CompilerParams(
            dimension_semantics=("arbitrary", "arbitrary"),
            vmem_limit_bytes=_vmem_limit([3 * _nbytes((seq, hd), F32), _nbytes((seq, LANES), F32),
                                          _nbytes((SUBLANES, seq), F32), _nbytes((seq, hd), BF16)],
                                         8 * _nbytes((Q_TILE, seq), F32))),
        name="forgetting_attention",
    )(z3, z3, z3, cum_col, cum_row, q_gain.reshape(depth, 1, hd), k_gain.reshape(depth, 1, hd))


def _router_body(x_ref, g_ref, rw_ref, rb_ref, ri_ref, rwt_ref, cnt_ref, carry_ref, *, bm):
    @pl.when(pl.program_id(0) == 0)
    def _():
        carry_ref[...] = jnp.zeros_like(carry_ref)

    x = x_ref[...]
    h = x * lax.rsqrt(jnp.mean(x * x, axis=-1, keepdims=True) + EPS) * g_ref[...]
    logits = jnp.dot(h, rw_ref[...], preferred_element_type=F32,
                     precision=lax.Precision.HIGHEST) + rb_ref[...]
    lane = lax.broadcasted_iota(jnp.int32, (bm, LANES), 1)
    logits = jnp.where(lane < N_EXPERTS, logits, -jnp.inf)
    m1 = jnp.max(logits, axis=-1, keepdims=True)
    i1 = jnp.min(jnp.where(logits == m1, lane, LANES), axis=-1, keepdims=True)
    rest = jnp.where(lane == i1, -jnp.inf, logits)
    m2 = jnp.max(rest, axis=-1, keepdims=True)
    i2 = jnp.min(jnp.where(rest == m2, lane, LANES), axis=-1, keepdims=True)
    e2 = jnp.exp(m2 - m1)
    w1 = 1.0 / (1.0 + e2)
    w2 = e2 / (1.0 + e2)

    onehot = ((lane == i1) | (lane == i2)).astype(BF16)
    row = lax.broadcasted_iota(jnp.int32, (bm, bm), 0)
    col = lax.broadcasted_iota(jnp.int32, (bm, bm), 1)
    tri = (row > col).astype(BF16)
    before = jnp.dot(tri, onehot, preferred_element_type=F32) + carry_ref[0:1, :]
    rank1 = jnp.sum(jnp.where(lane == i1, before, 0.0), axis=-1, keepdims=True).astype(jnp.int32)
    rank2 = jnp.sum(jnp.where(lane == i2, before, 0.0), axis=-1, keepdims=True).astype(jnp.int32)
    total = carry_ref[0:1, :] + jnp.sum(onehot.astype(F32), axis=0, keepdims=True)
    carry_ref[...] = jnp.broadcast_to(total, carry_ref.shape)
    cnt_ref[...] = jnp.broadcast_to(total, cnt_ref.shape)

    ri_ref[...] = jnp.where(lane == 0, i1, jnp.where(lane == 1, i2,
                            jnp.where(lane == 2, rank1, jnp.where(lane == 3, rank2, 0))))
    rwt_ref[...] = jnp.where(lane == 0, w1, jnp.where(lane == 1, w2, 0.0))


def _router(x2, gain, rw, rb, l):
    n, d = x2.shape
    bm = ROW_TILE
    return pl.pallas_call(
        functools.partial(_router_body, bm=bm),
        grid=(n // bm,),
        in_specs=[pl.BlockSpec((bm, d), lambda i: (i, 0)),
                  pl.BlockSpec((None, 1, d), lambda i: (l, 0, 0)),
                  pl.BlockSpec((d, LANES), lambda i: (0, 0)),
                  pl.BlockSpec((1, LANES), lambda i: (0, 0))],
        out_specs=[pl.BlockSpec((bm, LANES), lambda i: (i, 0)),
                   pl.BlockSpec((bm, LANES), lambda i: (i, 0)),
                   pl.BlockSpec((SUBLANES, LANES), lambda i: (0, 0))],
        out_shape=[jax.ShapeDtypeStruct((n, LANES), jnp.int32),
                   jax.ShapeDtypeStruct((n, LANES), F32),
                   jax.ShapeDtypeStruct((SUBLANES, LANES), F32)],
        scratch_shapes=[pltpu.VMEM((SUBLANES, LANES), F32)],
        compiler_params=pltpu.CompilerParams(dimension_semantics=("arbitrary",)),
        name="moe_router",
    )(x2, gain.reshape(gain.shape[0], 1, d), rw, rb)


def _row_copy(src_hbm, dst_vmem, sem, src_row, dst_row):
    return pltpu.make_async_copy(src_hbm.at[pl.ds(src_row, 1), :], dst_vmem.at[pl.ds(dst_row, 1), :], sem)


def _dispatch_body(tok_ref, ng_ref, x_hbm, g_ref, o_ref, buf_ref, sem, *, tg):
    i = pl.program_id(0)
    slot = i & 1
    n_live = ng_ref[0]

    def issue_tile(step, slot_):
        def issue(r, c):
            _row_copy(x_hbm, buf_ref.at[slot_], sem.at[slot_], tok_ref[step * tg + r], r).start()
            return c
        lax.fori_loop(0, tg, issue, 0, unroll=GATHER_UNROLL)

    @pl.when(i == 0)
    def _():
        issue_tile(0, 0)

    @pl.when(i + 1 < n_live)
    def _():
        issue_tile(i + 1, 1 - slot)

    @pl.when(i < n_live)
    def _():
        def drain(r, c):
            _row_copy(x_hbm, buf_ref.at[slot], sem.at[slot], 0, r).wait()
            return c
        lax.fori_loop(0, tg, drain, 0, unroll=GATHER_UNROLL)
        x = buf_ref[slot]
        h = x * lax.rsqrt(jnp.mean(x * x, axis=-1, keepdims=True) + EPS) * g_ref[...]
        o_ref[...] = h.astype(o_ref.dtype)

    @pl.when(i >= n_live)
    def _():
        o_ref[...] = jnp.zeros_like(o_ref)


def _dispatch(row_token, n_gather_tiles, x2, gain, l, rows):
    n, d = x2.shape
    tg = GATHER_TILE
    return pl.pallas_call(
        functools.partial(_dispatch_body, tg=tg),
        grid_spec=pltpu.PrefetchScalarGridSpec(
            num_scalar_prefetch=2,
            grid=(rows // tg,),
            in_specs=[pl.BlockSpec(memory_space=pl.ANY),
                      pl.BlockSpec((None, 1, d), lambda i, tok, ng: (l, 0, 0))],
            out_specs=pl.BlockSpec((tg, d), lambda i, tok, ng: (i, 0)),
            scratch_shapes=[pltpu.VMEM((2, tg, d), F32), pltpu.SemaphoreType.DMA((2,))]),
        out_shape=jax.ShapeDtypeStruct((rows, d), BF16),
        compiler_params=pltpu.CompilerParams(dimension_semantics=("arbitrary",)),
        name="moe_dispatch_gather",
    )(row_token, n_gather_tiles, x2, gain.reshape(gain.shape[0], 1, d))


def _moe_up_body(te_ref, nu_ref, a_ref, wg_ref, wu_ref, o_ref):
    used = pl.program_id(1) < nu_ref[0]

    @pl.when(used)
    def _():
        a = a_ref[...]
        g = jnp.dot(a, wg_ref[...].astype(BF16), preferred_element_type=F32)
        u = jnp.dot(a, wu_ref[...].astype(BF16), preferred_element_type=F32)
        o_ref[...] = ((g * _sigmoid(g)) * u).astype(o_ref.dtype)

    @pl.when(jnp.logical_not(used))
    def _():
        o_ref[...] = jnp.zeros_like(o_ref)


def _moe_down_body(te_ref, nu_ref, a_ref, wd_ref, o_ref):
    used = pl.program_id(1) < nu_ref[0]

    @pl.when(used)
    def _():
        o_ref[...] = jnp.dot(a_ref[...], wd_ref[...].astype(BF16), preferred_element_type=F32)

    @pl.when(jnp.logical_not(used))
    def _():
        o_ref[...] = jnp.zeros_like(o_ref)


def _tile_clamp(t, nu):
    return jnp.minimum(t, nu[0] - 1)


def _moe_up(tile_expert, n_used, xs, wg, wu, m):
    rows, d = xs.shape
    dff = wg.shape[-1]
    tm, bn = MOE_TILE, 1024
    wspec = pl.BlockSpec((None, None, d, bn), lambda j, t, te, nu: (m, te[_tile_clamp(t, nu)], 0, j))
    blocks = [_nbytes((tm, d), BF16), 2 * _nbytes((d, bn), F32), _nbytes((tm, bn), BF16)]
    temps = 2 * _nbytes((d, bn), BF16) + 3 * _nbytes((tm, bn), F32)
    return pl.pallas_call(
        _moe_up_body,
        grid_spec=pltpu.PrefetchScalarGridSpec(
            num_scalar_prefetch=2,
            grid=(dff // bn, rows // tm),
            in_specs=[pl.BlockSpec((tm, d), lambda j, t, te, nu: (_tile_clamp(t, nu), 0)), wspec, wspec],
            out_specs=pl.BlockSpec((tm, bn), lambda j, t, te, nu: (t, j))),
        out_shape=jax.ShapeDtypeStruct((rows, dff), BF16),
        compiler_params=pltpu.CompilerParams(dimension_semantics=("arbitrary", "arbitrary"),
                                             vmem_limit_bytes=_vmem_limit(blocks, temps)),
        name="moe_up_swiglu",
    )(tile_expert, n_used, xs, wg, wu)


def _moe_down(tile_expert, n_used, act, wd, m):
    rows, dff = act.shape
    d = wd.shape[-1]
    tm, bn = MOE_TILE, 512
    blocks = [_nbytes((tm, dff), BF16), _nbytes((dff, bn), F32), _nbytes((tm, bn), F32)]
    temps = _nbytes((dff, bn), BF16) + 2 * _nbytes((tm, bn), F32)
    return pl.pallas_call(
        _moe_down_body,
        grid_spec=pltpu.PrefetchScalarGridSpec(
            num_scalar_prefetch=2,
            grid=(d // bn, rows // tm),
            in_specs=[pl.BlockSpec((tm, dff), lambda j, t, te, nu: (_tile_clamp(t, nu), 0)),
                      pl.BlockSpec((None, None, dff, bn),
                                   lambda j, t, te, nu: (m, te[_tile_clamp(t, nu)], 0, j))],
            out_specs=pl.BlockSpec((tm, bn), lambda j, t, te, nu: (t, j))),
        out_shape=jax.ShapeDtypeStruct((rows, d), F32),
        compiler_params=pltpu.CompilerParams(dimension_semantics=("arbitrary", "arbitrary"),
                                             vmem_limit_bytes=_vmem_limit(blocks, temps)),
        name="moe_down",
    )(tile_expert, n_used, act, wd)


def _combine_body(pos_ref, y_hbm, x_ref, w_ref, o_ref, buf_ref, sem, *, tc, n_steps):
    i = pl.program_id(0)
    slot = i & 1

    def issue_tile(step, slot_):
        def issue(r, c):
            for k in range(TOP_K):
                _row_copy(y_hbm, buf_ref.at[slot_, k], sem.at[slot_],
                          pos_ref[(step * tc + r) * TOP_K + k], r).start()
            return c
        lax.fori_loop(0, tc, issue, 0, unroll=GATHER_UNROLL)

    @pl.when(i == 0)
    def _():
        issue_tile(0, 0)

    @pl.when(i + 1 < n_steps)
    def _():
        issue_tile(i + 1, 1 - slot)

    def drain(r, c):
        for k in range(TOP_K):
            _row_copy(y_hbm, buf_ref.at[slot, k], sem.at[slot], 0, r).wait()
        return c
    lax.fori_loop(0, tc, drain, 0, unroll=GATHER_UNROLL)
    w = w_ref[...]
    o_ref[...] = x_ref[...] + w[:, 0:1] * buf_ref[slot, 0] + w[:, 1:2] * buf_ref[slot, 1]


def _combine(pos_flat, y, x2, route_w):
    n, d = x2.shape
    tc = GATHER_TILE
    return pl.pallas_call(
        functools.partial(_combine_body, tc=tc, n_steps=n // tc),
        grid_spec=pltpu.PrefetchScalarGridSpec(
            num_scalar_prefetch=1,
            grid=(n // tc,),
            in_specs=[pl.BlockSpec(memory_space=pl.ANY),
                      pl.BlockSpec((tc, d), lambda i, pos: (i, 0)),
                      pl.BlockSpec((tc, LANES), lambda i, pos: (i, 0))],
            out_specs=pl.BlockSpec((tc, d), lambda i, pos: (i, 0)),
            scratch_shapes=[pltpu.VMEM((2, TOP_K, tc, d), F32), pltpu.SemaphoreType.DMA((2,))]),
        out_shape=jax.ShapeDtypeStruct((n, d), F32),
        compiler_params=pltpu.CompilerParams(
            dimension_semantics=("arbitrary",),
            vmem_limit_bytes=_vmem_limit([2 * _nbytes((tc, d), F32), _nbytes((tc, LANES), F32)],
                                         2 * TOP_K * _nbytes((tc, d), F32))),
        name="moe_combine_gather",
    )(pos_flat, y, x2, route_w)


def _routing_tables(route_i, counts, n_tokens, rows):
    tm = MOE_TILE
    expert = route_i[:, 0:TOP_K]
    rank = route_i[:, TOP_K:2 * TOP_K]
    cnt = counts[0, :N_EXPERTS].astype(jnp.int32)
    tiles_e = (cnt + (tm - 1)) // tm
    tile_end = jnp.cumsum(tiles_e)
    row_off = (tile_end - tiles_e) * tm
    pos = row_off[expert] + rank
    token = jnp.broadcast_to(jnp.arange(n_tokens, dtype=jnp.int32)[:, None], pos.shape)
    row_token = jnp.zeros((rows,), jnp.int32).at[pos.reshape(-1)].set(token.reshape(-1))
    tiles = jnp.arange(rows // tm, dtype=jnp.int32)
    tile_expert = jnp.minimum(jnp.sum(tiles[:, None] >= tile_end[None, :], axis=1),
                              N_EXPERTS - 1).astype(jnp.int32)
    n_used = tile_end[N_EXPERTS - 1:N_EXPERTS].astype(jnp.int32)
    return pos.reshape(-1).astype(jnp.int32), row_token, tile_expert, n_used


def _moe_ffn(x2, gain, l, router_w, router_b, wg, wu, wd, m):
    n, d = x2.shape
    rows = n * TOP_K + N_EXPERTS * MOE_TILE
    rw = jnp.pad(router_w[m], ((0, 0), (0, LANES - N_EXPERTS)))
    rb = jnp.pad(router_b[m], (0, LANES - N_EXPERTS)).reshape(1, LANES)
    route_i, route_w, counts = _router(x2, gain, rw, rb, l)
    pos_flat, row_token, tile_expert, n_used = _routing_tables(route_i, counts, n, rows)
    xs = _dispatch(row_token, n_used * (MOE_TILE // GATHER_TILE), x2, gain, l, rows)
    act = _moe_up(tile_expert, n_used, xs, wg, wu, m)
    y = _moe_down(tile_expert, n_used, act, wd, m)
    return _combine(pos_flat, y, x2, route_w)


def _dense_ffn(x2, gain, l, wg, wu, wd, m):
    n, d = x2.shape
    dff = wg.shape[-1]
    h = _rmsnorm(x2, gain, l)
    bm, bn = 1024, 512
    act = _mm("ffn_up_swiglu", n, dff, bm, bn,
              [(h, _a_spec(bm, d))],
              [(wg, _w_spec(d, bn, m)), (wu, _w_spec(d, bn, m))], [],
              [(0, 0), (0, 1)], _ep_swiglu, BF16)
    bm, bn = 512, 512
    return _mm("ffn_down_residual", n, d, bm, bn,
               [(act, _a_spec(bm, dff))],
               [(wd, _w_spec(dff, bn, m))],
               [(x2, pl.BlockSpec((bm, bn), lambda j, i: (i, j)))],
               [(0, 0)], _ep_residual, F32)


def _mixer(x2, l, batch, seq, norm_mix_g, w_in, lru_conv_w, lru_conv_b, lru_wa, lru_ba, lru_wx, lru_bx,
           lru_lambda, fox_bf, q_norm_g, k_norm_g, sc_conv_w, w_branch_lru, w_branch_att, w_branch_conv,
           w_merge, b_merge, w_out):
    n, d = x2.shape
    depth = w_in.shape[0]
    u = _rmsnorm(x2, norm_mix_g, l)

    w_l = w_in[l]
    w_main = w_l[:, :COL_MAIN].astype(BF16)[None]
    w_tail = w_l[:, COL_TAIL:].astype(BF16)[None]
    bm, bn = 1024, 1024
    z = _mm("in_proj_main", n, COL_MAIN, bm, bn, [(u, _a_spec(bm, d))], [(w_main, _w_spec(d, bn, 0))], [],
            [(0, 0)], _ep_identity, F32)
    zt = _mm("in_proj_tail", n, 3 * CONV_WIDTH, bm, bn, [(u, _a_spec(bm, d))],
             [(w_tail, _w_spec(d, bn, 0))], [], [(0, 0)], _ep_identity, F32)
    z3 = z.reshape(batch, seq, COL_MAIN)
    zt3 = zt.reshape(batch, seq, 3 * CONV_WIDTH)

    wf = jnp.pad(w_l[:, COL_F:COL_TAIL], ((0, 0), (0, LANES - ATT_HEADS))).astype(BF16)
    bf = jnp.pad(fox_bf[l], (0, LANES - ATT_HEADS)).reshape(1, LANES)
    cum = _fgate(u, wf, bf, batch, seq)[:, :ATT_HEADS].reshape(batch, seq, ATT_HEADS)
    cum_h = cum.transpose(0, 2, 1)
    cum_col = cum_h.reshape(batch, ATT_HEADS, seq, 1)
    cum_row = cum_h.reshape(batch, ATT_HEADS, 1, seq)

    y_lru = _lru_branch(z3, lru_conv_w, lru_conv_b, lru_wa, lru_ba, lru_wx, lru_bx, lru_lambda, l)
    y_att = _attention(z3, cum_col, cum_row, q_norm_g, k_norm_g, l)
    y_conv = _sconv_branch(zt3, sc_conv_w, l)

    bm, bn = 512, 512
    nb = d // bn
    ws = functools.partial(_w_spec, single_buffer=True)
    merged = _mm(
        "gated_merge", n, d, bm, bn,
        [(u, _a_spec(bm, d)), (y_lru.reshape(n, LRU_WIDTH), _a_spec(bm, LRU_WIDTH)),
         (y_att.reshape(n, ATT_WIDTH), _a_spec(bm, ATT_WIDTH)),
         (y_conv.reshape(n, CONV_WIDTH), _a_spec(bm, CONV_WIDTH))],
        [(w_merge, ws(d, bn, l, 0)), (w_merge, ws(d, bn, l, nb)), (w_merge, ws(d, bn, l, 2 * nb)),
         (w_branch_lru, ws(LRU_WIDTH, bn, l)), (w_branch_att, ws(ATT_WIDTH, bn, l)),
         (w_branch_conv, ws(CONV_WIDTH, bn, l))],
        [(b_merge.reshape(depth, 1, -1), _row_spec(bn, l, g * nb)) for g in range(3)],
        [(0, 0), (0, 1), (0, 2), (1, 3), (2, 4), (3, 5)], _ep_merge, BF16)

    bm, bn = 256, d
    return _mm("out_proj_residual", n, d, bm, bn, [(merged, _a_spec(bm, d))], [(w_out, ws(d, bn, l))],
               [(x2, pl.BlockSpec((bm, bn), lambda j, i: (i, j)))], [(0, 0)], _ep_residual, F32)


def kernel(x, norm_mix_g, w_in, lru_conv_w, lru_conv_b, lru_wa, lru_ba, lru_wx, lru_bx, lru_lambda, fox_bf, q_norm_g, k_norm_g, sc_conv_w, w_branch_lru, w_branch_att, w_branch_conv, w_merge, b_merge, w_out, norm_ffn_g, ffn_wg, ffn_wu, ffn_wd, router_w, router_b, moe_wg, moe_wu, moe_wd):
    batch, seq, d = x.shape
    depth = w_in.shape[0]
    x2 = x.reshape(batch * seq, d)
    for l in range(depth):
        x2 = _mixer(x2, l, batch, seq, norm_mix_g, w_in, lru_conv_w, lru_conv_b, lru_wa, lru_ba, lru_wx,
                    lru_bx, lru_lambda, fox_bf, q_norm_g, k_norm_g, sc_conv_w, w_branch_lru, w_branch_att,
                    w_branch_conv, w_merge, b_merge, w_out)
        if l % 2 == 0:
            x2 = _dense_ffn(x2, norm_ffn_g, l, ffn_wg, ffn_wu, ffn_wd, l // 2)
        else:
            x2 = _moe_ffn(x2, norm_ffn_g, l, router_w, router_b, moe_wg, moe_wu, moe_wd, l // 2)
    return x2.reshape(batch, seq, d)
```

```python
import functools

import jax
import jax.numpy as jnp
from jax import lax
from jax.experimental import pallas as pl
from jax.experimental.pallas import tpu as pltpu

F32 = jnp.float32
BF16 = jnp.bfloat16

D_MODEL = 2048
LRU_WIDTH = 1024
LRU_BLOCKS = 8
LRU_BLOCK = 128
LRU_CONV = 4
LRU_C = 8.0
ATT_HEADS = 8
ATT_HEAD_DIM = 128
ATT_WIDTH = 1024
CONV_WIDTH = 1024
CONV_K = 3
N_EXPERTS = 8
TOP_K = 2
EPS = 1e-6
COL_MAIN = 2 * LRU_WIDTH + 3 * ATT_WIDTH
COL_F = COL_MAIN
COL_TAIL = COL_MAIN + ATT_HEADS

LANES = 128
SUBLANES = 8
V7X_VMEM_BUDGET = 56 * 1024 * 1024

ROW_TILE = 512
SEQ_TILE = 256
Q_TILE = 256
MOE_TILE = 512
GATHER_TILE = 256
GATHER_UNROLL = 8


def _vmem_limit(block_bytes, temp_bytes=0, single_bytes=0):
    need = 2 * sum(block_bytes) + single_bytes + temp_bytes + (4 << 20)
    return int(min(max(need, 16 << 20), V7X_VMEM_BUDGET))


def _nbytes(shape, dtype):
    n = 1
    for s in shape:
        n *= s
    return n * jnp.dtype(dtype).itemsize


def _softplus(y):
    return jnp.maximum(y, 0.0) + jnp.log1p(jnp.exp(-jnp.abs(y)))


def _sigmoid(y):
    return 1.0 / (1.0 + jnp.exp(-y))


def _rmsnorm_body(x_ref, g_ref, o_ref):
    x = x_ref[...]
    y = x * lax.rsqrt(jnp.mean(x * x, axis=-1, keepdims=True) + EPS)
    o_ref[...] = (y * g_ref[...]).astype(o_ref.dtype)


def _rmsnorm(x2, gain, l):
    n, d = x2.shape
    return pl.pallas_call(
        _rmsnorm_body,
        grid=(n // ROW_TILE,),
        in_specs=[pl.BlockSpec((ROW_TILE, d), lambda i: (i, 0)),
                  pl.BlockSpec((None, 1, d), lambda i: (l, 0, 0))],
        out_specs=pl.BlockSpec((ROW_TILE, d), lambda i: (i, 0)),
        out_shape=jax.ShapeDtypeStruct((n, d), BF16),
        compiler_params=pltpu.CompilerParams(dimension_semantics=("arbitrary",)),
        name="rmsnorm",
    )(x2, gain.reshape(gain.shape[0], 1, d))


def _mm_body(*refs, pairs, n_a, n_w, n_e, epilogue):
    a_refs = refs[:n_a]
    w_refs = refs[n_a:n_a + n_w]
    e_refs = refs[n_a + n_w:n_a + n_w + n_e]
    o_ref = refs[-1]
    accs = [jnp.dot(a_refs[ai][...], w_refs[wi][...].astype(BF16), preferred_element_type=F32)
            for ai, wi in pairs]
    o_ref[...] = epilogue(accs, [e[...] for e in e_refs]).astype(o_ref.dtype)


def _mm(name, m, n, bm, bn, a_ops, w_ops, e_ops, pairs, epilogue, out_dtype):
    arrays = [a for a, _ in a_ops] + [w for w, _ in w_ops] + [e for e, _ in e_ops]
    specs = [s for _, s in a_ops] + [s for _, s in w_ops] + [s for _, s in e_ops]
    blocks, single = [], 0
    for arr, spec in a_ops + w_ops + e_ops:
        nb = _nbytes([s for s in spec.block_shape if s is not None], arr.dtype)
        if spec.pipeline_mode is not None and spec.pipeline_mode.buffer_count == 1:
            single += nb
        else:
            blocks.append(nb)
    blocks.append(_nbytes((bm, bn), out_dtype))
    w_bf16 = sum(_nbytes([s for s in spec.block_shape if s is not None], BF16)
                 for w, spec in w_ops if w.dtype != BF16)
    temps = w_bf16 + len(pairs) * _nbytes((bm, bn), F32)
    body = functools.partial(_mm_body, pairs=tuple(pairs), n_a=len(a_ops), n_w=len(w_ops),
                             n_e=len(e_ops), epilogue=epilogue)
    return pl.pallas_call(
        body,
        grid=(pl.cdiv(n, bn), m // bm),
        in_specs=specs,
        out_specs=pl.BlockSpec((bm, bn), lambda j, i: (i, j)),
        out_shape=jax.ShapeDtypeStruct((m, n), out_dtype),
        compiler_params=pltpu.CompilerParams(
            dimension_semantics=("arbitrary", "arbitrary"),
            vmem_limit_bytes=_vmem_limit(blocks, temps, single)),
        name=name,
    )(*arrays)


def _a_spec(bm, k):
    return pl.BlockSpec((bm, k), lambda j, i: (i, 0))


def _w_spec(k, bn, l, col_block0=0, single_buffer=False):
    mode = pl.Buffered(1) if single_buffer else None
    return pl.BlockSpec((None, k, bn), lambda j, i: (l, 0, j + col_block0), pipeline_mode=mode)


def _row_spec(bn, l, col_block0=0):
    return pl.BlockSpec((None, 1, bn), lambda j, i: (l, 0, j + col_block0))


def _ep_identity(accs, extras):
    return accs[0]


def _ep_residual(accs, extras):
    return extras[0] + accs[0]


def _ep_swiglu(accs, extras):
    g, u = accs
    return (g * _sigmoid(g)) * u


def _ep_merge(accs, extras):
    g0, g1, g2, p0, p1, p2 = accs
    b0, b1, b2 = extras
    return _sigmoid(g0 + b0) * p0 + _sigmoid(g1 + b1) * p1 + _sigmoid(g2 + b2) * p2


def _fgate_body(u_ref, wf_ref, bf_ref, cum_ref, carry_ref, *, ts):
    @pl.when(pl.program_id(1) == 0)
    def _():
        carry_ref[...] = jnp.zeros_like(carry_ref)

    f = jnp.dot(u_ref[...], wf_ref[...], preferred_element_type=F32) + bf_ref[...]
    lf = jnp.minimum(f, 0.0) - jnp.log1p(jnp.exp(-jnp.abs(f)))
    row = lax.broadcasted_iota(jnp.int32, (ts, ts), 0)
    col = lax.broadcasted_iota(jnp.int32, (ts, ts), 1)
    tri = (row >= col).astype(BF16)
    hi = lf.astype(BF16)
    r1 = lf - hi.astype(F32)
    mid = r1.astype(BF16)
    lo = (r1 - mid.astype(F32)).astype(BF16)
    cum = (jnp.dot(tri, hi, preferred_element_type=F32)
           + jnp.dot(tri, mid, preferred_element_type=F32)
           + jnp.dot(tri, lo, preferred_element_type=F32)) + carry_ref[0:1, :]
    cum_ref[...] = cum
    carry_ref[...] = jnp.broadcast_to(cum[ts - 1:ts, :], carry_ref.shape)


def _fgate(u2, wf, bf, batch, seq):
    ts = ROW_TILE
    nsb = seq // ts
    return pl.pallas_call(
        functools.partial(_fgate_body, ts=ts),
        grid=(batch, nsb),
        in_specs=[pl.BlockSpec((ts, D_MODEL), lambda b, s: (b * nsb + s, 0)),
                  pl.BlockSpec((D_MODEL, LANES), lambda b, s: (0, 0)),
                  pl.BlockSpec((1, LANES), lambda b, s: (0, 0))],
        out_specs=pl.BlockSpec((ts, LANES), lambda b, s: (b * nsb + s, 0)),
        out_shape=jax.ShapeDtypeStruct((batch * seq, LANES), F32),
        scratch_shapes=[pltpu.VMEM((SUBLANES, LANES), F32)],
        compiler_params=pltpu.CompilerParams(dimension_semantics=("arbitrary", "arbitrary")),
        name="fgate_cumsum",
    )(u2, wf, bf)


def _shifted(ext_ref, k, ts):
    return ext_ref[pl.ds(SUBLANES - k, ts), :]


def _lru_body(x_ref, g_ref, cw_ref, cb_ref, wa_ref, ba_ref, wx_ref, bx_ref, lam_ref, y_ref,
              ext_ref, h_ref, *, ts):
    @pl.when(pl.program_id(1) == 0)
    def _():
        ext_ref[0:SUBLANES, :] = jnp.zeros((SUBLANES, LRU_WIDTH), F32)
        h_ref[...] = jnp.zeros_like(h_ref)

    x = x_ref[...]
    ext_ref[pl.ds(SUBLANES, ts), :] = x
    cw = cw_ref[...]
    xc = x * cw[LRU_CONV - 1:LRU_CONV, :] + cb_ref[...]
    for k in range(1, LRU_CONV):
        xc = xc + _shifted(ext_ref, k, ts) * cw[LRU_CONV - 1 - k:LRU_CONV - k, :]
    ext_ref[0:SUBLANES, :] = x[ts - SUBLANES:ts, :]

    xcb = xc.astype(BF16)
    ra, ia = [], []
    for n in range(LRU_BLOCKS):
        blk = xcb[:, n * LRU_BLOCK:(n + 1) * LRU_BLOCK]
        ra.append(jnp.dot(blk, wa_ref[n].astype(BF16), preferred_element_type=F32))
        ia.append(jnp.dot(blk, wx_ref[n].astype(BF16), preferred_element_type=F32))
    r = _sigmoid(jnp.concatenate(ra, axis=1) + ba_ref[...])
    i = _sigmoid(jnp.concatenate(ia, axis=1) + bx_ref[...])
    log_a = (-LRU_C) * r * _softplus(-lam_ref[...])
    a = jnp.exp(log_a)
    th = jnp.tanh(log_a)
    b = jnp.sqrt(-2.0 * th / (1.0 - th)) * (i * xc)

    rowm = lax.broadcasted_iota(jnp.int32, (ts, LRU_WIDTH), 0) & (SUBLANES - 1)
    for d in (1, 2, 4):
        a_sh = pltpu.roll(a, d, 0)
        b_sh = pltpu.roll(b, d, 0)
        m = rowm >= d
        b = jnp.where(m, a * b_sh + b, b)
        a = jnp.where(m, a * a_sh, a)
    h = h_ref[0:1, :]
    outs = []
    for c in range(ts // SUBLANES):
        sl = slice(c * SUBLANES, (c + 1) * SUBLANES)
        hc = a[sl, :] * h + b[sl, :]
        outs.append(hc)
        h = hc[SUBLANES - 1:SUBLANES, :]
    h_ref[...] = jnp.broadcast_to(h, h_ref.shape)
    hs = jnp.concatenate(outs, axis=0)
    y_ref[...] = (hs * jax.nn.gelu(g_ref[...])).astype(y_ref.dtype)


def _lru_branch(z3, conv_w, conv_b, w_a, b_a, w_x, b_x, lam, l):
    batch, seq, _ = z3.shape
    ts = SEQ_TILE
    depth = conv_w.shape[0]
    vec = lambda p: p.reshape(depth, 1, LRU_WIDTH)
    vspec = pl.BlockSpec((None, 1, LRU_WIDTH), lambda b, s: (l, 0, 0))
    wspec = pl.BlockSpec((None, LRU_BLOCKS, LRU_BLOCK, LRU_BLOCK), lambda b, s: (l, 0, 0, 0))
    return pl.pallas_call(
        functools.partial(_lru_body, ts=ts),
        grid=(batch, seq // ts),
        in_specs=[pl.BlockSpec((None, ts, LRU_WIDTH), lambda b, s: (b, s, 0)),
                  pl.BlockSpec((None, ts, LRU_WIDTH), lambda b, s: (b, s, 1)),
                  pl.BlockSpec((None, LRU_CONV, LRU_WIDTH), lambda b, s: (l, 0, 0)),
                  vspec, wspec, vspec, wspec, vspec, vspec],
        out_specs=pl.BlockSpec((None, ts, LRU_WIDTH), lambda b, s: (b, s, 0)),
        out_shape=jax.ShapeDtypeStruct((batch, seq, LRU_WIDTH), BF16),
        scratch_shapes=[pltpu.VMEM((ts + SUBLANES, LRU_WIDTH), F32),
                        pltpu.VMEM((SUBLANES, LRU_WIDTH), F32)],
        compiler_params=pltpu.CompilerParams(dimension_semantics=("arbitrary", "arbitrary")),
        name="rglru_branch",
    )(z3, z3, conv_w, vec(conv_b), w_a, vec(b_a), w_x, vec(b_x), vec(lam))


def _sconv_body(b_ref, c_ref, h_ref, w_ref, y_ref, ext_ref, *, ts):
    @pl.when(pl.program_id(1) == 0)
    def _():
        ext_ref[0:SUBLANES, :] = jnp.zeros((SUBLANES, CONV_WIDTH), F32)

    ch = c_ref[...] * h_ref[...]
    ext_ref[pl.ds(SUBLANES, ts), :] = ch
    w = w_ref[...]
    y = ch * w[CONV_K - 1:CONV_K, :]
    for k in range(1, CONV_K):
        y = y + _shifted(ext_ref, k, ts) * w[CONV_K - 1 - k:CONV_K - k, :]
    ext_ref[0:SUBLANES, :] = ch[ts - SUBLANES:ts, :]
    y_ref[...] = (b_ref[...] * y).astype(y_ref.dtype)


def _sconv_branch(zt3, conv_w, l):
    batch, seq, _ = zt3.shape
    ts = SEQ_TILE
    col = lambda c: pl.BlockSpec((None, ts, CONV_WIDTH), lambda b, s: (b, s, c))
    return pl.pallas_call(
        functools.partial(_sconv_body, ts=ts),
        grid=(batch, seq // ts),
        in_specs=[col(0), col(1), col(2),
                  pl.BlockSpec((None, CONV_K, CONV_WIDTH), lambda b, s: (l, 0, 0))],
        out_specs=pl.BlockSpec((None, ts, CONV_WIDTH), lambda b, s: (b, s, 0)),
        out_shape=jax.ShapeDtypeStruct((batch, seq, CONV_WIDTH), BF16),
        scratch_shapes=[pltpu.VMEM((ts + SUBLANES, CONV_WIDTH), F32)],
        compiler_params=pltpu.CompilerParams(dimension_semantics=("arbitrary", "arbitrary")),
        name="short_conv_branch",
    )(zt3, zt3, zt3, conv_w)


def _head_norm(t, gain):
    return t * lax.rsqrt(jnp.mean(t * t, axis=-1, keepdims=True) + EPS) * gain


def _attn_body(q_ref, k_ref, v_ref, cc_ref, cr_ref, qg_ref, kg_ref, o_ref, *, seq, tq):
    log2e = 1.4426950408889634
    qn = (_head_norm(q_ref[...], qg_ref[...]) * (ATT_HEAD_DIM ** -0.5 * log2e)).astype(BF16)
    kn = _head_norm(k_ref[...], kg_ref[...]).astype(BF16)
    v = v_ref[...].astype(BF16)
    cum_k = cr_ref[...] * log2e
    causal = (lax.broadcasted_iota(jnp.int32, (tq, tq), 0) >= lax.broadcasted_iota(jnp.int32, (tq, tq), 1))
    nt = (((1,), (1,)), ((), ()))
    def scores(qi):
        q0, kv = qi * tq, (qi + 1) * tq
        return lax.dot_general(qn[q0:kv, :], kn[:kv, :], nt, preferred_element_type=F32) - cum_k[:, :kv]

    n_q = seq // tq
    t_next = scores(0)
    for qi in range(n_q):
        q0, kv = qi * tq, (qi + 1) * tq
        cum_q = cc_ref[q0:kv, :] * log2e
        t = t_next
        if qi + 1 < n_q:
            t_next = scores(qi + 1)
        td = jnp.where(causal, t[:, q0:kv], -jnp.inf)
        t = td if qi == 0 else jnp.concatenate([t[:, :q0], td], axis=1)
        mt = jnp.max(t, axis=-1, keepdims=True)
        off = cum_q - (mt + cum_q)
        p = jnp.exp2(t + off)
        denom = jnp.sum(p, axis=-1, keepdims=True)
        o = jnp.dot(p.astype(BF16), v[:kv, :], preferred_element_type=F32)
        o_ref[q0:kv, :] = (o / denom).astype(o_ref.dtype)


def _attention(z3, cum_col, cum_row, q_gain, k_gain, l):
    batch, seq, _ = z3.shape
    depth = q_gain.shape[0]
    hd = ATT_HEAD_DIM
    head = lambda base: pl.BlockSpec((None, seq, hd), lambda b, h: (b, 0, base + h))
    gspec = pl.BlockSpec((None, 1, hd), lambda b, h: (l, 0, 0))
    q_base = 2 * LRU_WIDTH // hd
    return pl.pallas_call(
        functools.partial(_attn_body, seq=seq, tq=Q_TILE),
        grid=(batch, ATT_HEADS),
        in_specs=[head(q_base), head(q_base + ATT_HEADS), head(q_base + 2 * ATT_HEADS),
                  pl.BlockSpec((None, None, seq, 1), lambda b, h: (b, h, 0, 0)),
                  pl.BlockSpec((None, None, 1, seq), lambda b, h: (b, h, 0, 0)),
                  gspec, gspec],
        out_specs=pl.BlockSpec((None, seq, hd), lambda b, h: (b, 0, h)),
        out_shape=jax.ShapeDtypeStruct((batch, seq, ATT_WIDTH), BF16),
        compiler_params=pltpu.CompilerParams(
            dimension_semantics=("arbitrary", "arbitrary"),
            vmem_limit_bytes=_vmem_limit([3 * _nbytes((seq, hd), F32), _nbytes((seq, LANES), F32),
                                          _nbytes((SUBLANES, seq), F32), _nbytes((seq, hd), BF16)],
                                         8 * _nbytes((Q_TILE, seq), F32))),
        name="forgetting_attention",
    )(z3, z3, z3, cum_col, cum_row, q_gain.reshape(depth, 1, hd), k_gain.reshape(depth, 1, hd))


def _router_body(x_ref, g_ref, rw_ref, rb_ref, ri_ref, rwt_ref, cnt_ref, carry_ref, *, bm):
    @pl.when(pl.program_id(0) == 0)
    def _():
        carry_ref[...] = jnp.zeros_like(carry_ref)

    x = x_ref[...]
    h = x * lax.rsqrt(jnp.mean(x * x, axis=-1, keepdims=True) + EPS) * g_ref[...]
    logits = jnp.dot(h, rw_ref[...], preferred_element_type=F32,
                     precision=lax.Precision.HIGHEST) + rb_ref[...]
    lane = lax.broadcasted_iota(jnp.int32, (bm, LANES), 1)
    logits = jnp.where(lane < N_EXPERTS, logits, -jnp.inf)
    m1 = jnp.max(logits, axis=-1, keepdims=True)
    i1 = jnp.min(jnp.where(logits == m1, lane, LANES), axis=-1, keepdims=True)
    rest = jnp.where(lane == i1, -jnp.inf, logits)
    m2 = jnp.max(rest, axis=-1, keepdims=True)
    i2 = jnp.min(jnp.where(rest == m2, lane, LANES), axis=-1, keepdims=True)
    e2 = jnp.exp(m2 - m1)
    w1 = 1.0 / (1.0 + e2)
    w2 = e2 / (1.0 + e2)

    onehot = ((lane == i1) | (lane == i2)).astype(BF16)
    row = lax.broadcasted_iota(jnp.int32, (bm, bm), 0)
    col = lax.broadcasted_iota(jnp.int32, (bm, bm), 1)
    tri = (row > col).astype(BF16)
    before = jnp.dot(tri, onehot, preferred_element_type=F32) + carry_ref[0:1, :]
    rank1 = jnp.sum(jnp.where(lane == i1, before, 0.0), axis=-1, keepdims=True).astype(jnp.int32)
    rank2 = jnp.sum(jnp.where(lane == i2, before, 0.0), axis=-1, keepdims=True).astype(jnp.int32)
    total = carry_ref[0:1, :] + jnp.sum(onehot.astype(F32), axis=0, keepdims=True)
    carry_ref[...] = jnp.broadcast_to(total, carry_ref.shape)
    cnt_ref[...] = jnp.broadcast_to(total, cnt_ref.shape)

    ri_ref[...] = jnp.where(lane == 0, i1, jnp.where(lane == 1, i2,
                            jnp.where(lane == 2, rank1, jnp.where(lane == 3, rank2, 0))))
    rwt_ref[...] = jnp.where(lane == 0, w1, jnp.where(lane == 1, w2, 0.0))


def _router(x2, gain, rw, rb, l):
    n, d = x2.shape
    bm = ROW_TILE
    return pl.pallas_call(
        functools.partial(_router_body, bm=bm),
        grid=(n // bm,),
        in_specs=[pl.BlockSpec((bm, d), lambda i: (i, 0)),
                  pl.BlockSpec((None, 1, d), lambda i: (l, 0, 0)),
                  pl.BlockSpec((d, LANES), lambda i: (0, 0)),
                  pl.BlockSpec((1, LANES), lambda i: (0, 0))],
        out_specs=[pl.BlockSpec((bm, LANES), lambda i: (i, 0)),
                   pl.BlockSpec((bm, LANES), lambda i: (i, 0)),
                   pl.BlockSpec((SUBLANES, LANES), lambda i: (0, 0))],
        out_shape=[jax.ShapeDtypeStruct((n, LANES), jnp.int32),
                   jax.ShapeDtypeStruct((n, LANES), F32),
                   jax.ShapeDtypeStruct((SUBLANES, LANES), F32)],
        scratch_shapes=[pltpu.VMEM((SUBLANES, LANES), F32)],
        compiler_params=pltpu.CompilerParams(dimension_semantics=("arbitrary",)),
        name="moe_router",
    )(x2, gain.reshape(gain.shape[0], 1, d), rw, rb)


def _row_copy(src_hbm, dst_vmem, sem, src_row, dst_row):
    return pltpu.make_async_copy(src_hbm.at[pl.ds(src_row, 1), :], dst_vmem.at[pl.ds(dst_row, 1), :], sem)


def _dispatch_body(tok_ref, ng_ref, x_hbm, g_ref, o_ref, buf_ref, sem, *, tg):
    i = pl.program_id(0)
    slot = i & 1
    n_live = ng_ref[0]

    def issue_tile(step, slot_):
        def issue(r, c):
            _row_copy(x_hbm, buf_ref.at[slot_], sem.at[slot_], tok_ref[step * tg + r], r).start()
            return c
        lax.fori_loop(0, tg, issue, 0, unroll=GATHER_UNROLL)

    @pl.when(i == 0)
    def _():
        issue_tile(0, 0)

    @pl.when(i + 1 < n_live)
    def _():
        issue_tile(i + 1, 1 - slot)

    @pl.when(i < n_live)
    def _():
        def drain(r, c):
            _row_copy(x_hbm, buf_ref.at[slot], sem.at[slot], 0, r).wait()
            return c
        lax.fori_loop(0, tg, drain, 0, unroll=GATHER_UNROLL)
        x = buf_ref[slot]
        h = x * lax.rsqrt(jnp.mean(x * x, axis=-1, keepdims=True) + EPS) * g_ref[...]
        o_ref[...] = h.astype(o_ref.dtype)

    @pl.when(i >= n_live)
    def _():
        o_ref[...] = jnp.zeros_like(o_ref)


def _dispatch(row_token, n_gather_tiles, x2, gain, l, rows):
    n, d = x2.shape
    tg = GATHER_TILE
    return pl.pallas_call(
        functools.partial(_dispatch_body, tg=tg),
        grid_spec=pltpu.PrefetchScalarGridSpec(
            num_scalar_prefetch=2,
            grid=(rows // tg,),
            in_specs=[pl.BlockSpec(memory_space=pl.ANY),
                      pl.BlockSpec((None, 1, d), lambda i, tok, ng: (l, 0, 0))],
            out_specs=pl.BlockSpec((tg, d), lambda i, tok, ng: (i, 0)),
            scratch_shapes=[pltpu.VMEM((2, tg, d), F32), pltpu.SemaphoreType.DMA((2,))]),
        out_shape=jax.ShapeDtypeStruct((rows, d), BF16),
        compiler_params=pltpu.CompilerParams(dimension_semantics=("arbitrary",)),
        name="moe_dispatch_gather",
    )(row_token, n_gather_tiles, x2, gain.reshape(gain.shape[0], 1, d))


N_MOE_TABLES = 7


def _weight_copies(w_hbms, wbuf, sem, m, expert, j, slot, bn):
    col = pl.multiple_of(j * bn, bn)
    return [pltpu.make_async_copy(w.at[m, expert, :, pl.ds(col, bn)], wbuf.at[slot, k], sem.at[slot, k])
            for k, w in enumerate(w_hbms)]


def _moe_step(tables, w_hbms, wbuf, sem, o_ref, compute, *, tm, bn, m, n_j):
    te_ref, tr_ref, nu_ref, fl_ref, gi_ref, ne_ref, ng_ref = tables
    j, t = pl.program_id(0), pl.program_id(1)
    used = t < nu_ref[0]
    half = tr_ref[t] <= tm // 2
    n_groups = ng_ref[0]
    gidx = j * n_groups + gi_ref[t]
    slot = gidx & 1

    @pl.when(used & (fl_ref[t] == 1))
    def _():
        @pl.when(gidx == 0)
        def _():
            for c in _weight_copies(w_hbms, wbuf, sem, m, te_ref[t], 0, 0, bn):
                c.start()
        for c in _weight_copies(w_hbms, wbuf, sem, m, te_ref[t], j, slot, bn):
            c.wait()
        j_next = j + jnp.where(gi_ref[t] == n_groups - 1, 1, 0)

        @pl.when(j_next < n_j)
        def _():
            for c in _weight_copies(w_hbms, wbuf, sem, m, ne_ref[t], j_next, 1 - slot, bn):
                c.start()

    def run(rows):
        compute(rows, slot)
        if rows < tm:
            o_ref[rows:tm, :] = jnp.zeros((tm - rows, o_ref.shape[1]), o_ref.dtype)

    pl.when(used & jnp.logical_not(half))(lambda: run(tm))
    pl.when(used & half)(lambda: run(tm // 2))

    @pl.when(jnp.logical_not(used))
    def _():
        o_ref[...] = jnp.zeros_like(o_ref)


def _moe_up_body(*refs, **cfg):
    tables = refs[:N_MOE_TABLES]
    a_ref, wg_hbm, wu_hbm, o_ref, wbuf, sem = refs[N_MOE_TABLES:]

    def compute(rows, slot):
        a = a_ref[0:rows, :]
        g = jnp.dot(a, wbuf[slot, 0].astype(BF16), preferred_element_type=F32)
        u = jnp.dot(a, wbuf[slot, 1].astype(BF16), preferred_element_type=F32)
        o_ref[0:rows, :] = ((g * _sigmoid(g)) * u).astype(o_ref.dtype)

    _moe_step(tables, (wg_hbm, wu_hbm), wbuf, sem, o_ref, compute, **cfg)


def _moe_down_body(*refs, **cfg):
    tables = refs[:N_MOE_TABLES]
    a_ref, wd_hbm, o_ref, wbuf, sem = refs[N_MOE_TABLES:]

    def compute(rows, slot):
        o_ref[0:rows, :] = jnp.dot(a_ref[0:rows, :], wbuf[slot, 0].astype(BF16), preferred_element_type=F32)

    _moe_step(tables, (wd_hbm,), wbuf, sem, o_ref, compute, **cfg)


def _tile_clamp(t, nu):
    return jnp.maximum(jnp.minimum(t, nu[0] - 1), 0)


def _moe_grouped(name, body, tables, a, weights, m, bn, out_dtype, n_acc):
    rows, k = a.shape
    n_out = weights[0].shape[-1]
    tm = MOE_TILE
    n_j = n_out // bn
    a_map = lambda j, t, te, tr, nu, fl, gi, ne, ng: (_tile_clamp(t, nu), 0)
    o_map = lambda j, t, te, tr, nu, fl, gi, ne, ng: (t, j)
    blocks = [_nbytes((tm, k), BF16), _nbytes((tm, bn), out_dtype)]
    wbuf_bytes = 2 * len(weights) * _nbytes((k, bn), F32)
    temps = len(weights) * _nbytes((k, bn), BF16) + n_acc * _nbytes((tm, bn), F32)
    return pl.pallas_call(
        functools.partial(body, tm=tm, bn=bn, m=m, n_j=n_j),
        grid_spec=pltpu.PrefetchScalarGridSpec(
            num_scalar_prefetch=N_MOE_TABLES,
            grid=(n_j, rows // tm),
            in_specs=[pl.BlockSpec((tm, k), a_map)] + [pl.BlockSpec(memory_space=pl.ANY)] * len(weights),
            out_specs=pl.BlockSpec((tm, bn), o_map),
            scratch_shapes=[pltpu.VMEM((2, len(weights), k, bn), F32),
                            pltpu.SemaphoreType.DMA((2, len(weights)))]),
        out_shape=jax.ShapeDtypeStruct((rows, n_out), out_dtype),
        compiler_params=pltpu.CompilerParams(dimension_semantics=("arbitrary", "arbitrary"),
                                             vmem_limit_bytes=_vmem_limit(blocks, temps, wbuf_bytes)),
        name=name,
    )(*tables, a, *weights)


def _moe_up(tables, xs, wg, wu, m):
    return _moe_grouped("moe_up_swiglu", _moe_up_body, tables, xs, (wg, wu), m, 1024, BF16, 3)


def _moe_down(tables, act, wd, m):
    return _moe_grouped("moe_down", _moe_down_body, tables, act, (wd,), m, 512, F32, 2)


def _combine_body(pos_ref, y_hbm, x_ref, w_ref, o_ref, buf_ref, sem, *, tc, n_steps):
    i = pl.program_id(0)
    slot = i & 1

    def issue_tile(step, slot_):
        def issue(r, c):
            for k in range(TOP_K):
                _row_copy(y_hbm, buf_ref.at[slot_, k], sem.at[slot_],
                          pos_ref[(step * tc + r) * TOP_K + k], r).start()
            return c
        lax.fori_loop(0, tc, issue, 0, unroll=GATHER_UNROLL)

    @pl.when(i == 0)
    def _():
        issue_tile(0, 0)

    @pl.when(i + 1 < n_steps)
    def _():
        issue_tile(i + 1, 1 - slot)

    def drain(r, c):
        for k in range(TOP_K):
            _row_copy(y_hbm, buf_ref.at[slot, k], sem.at[slot], 0, r).wait()
        return c
    lax.fori_loop(0, tc, drain, 0, unroll=GATHER_UNROLL)
    w = w_ref[...]
    o_ref[...] = x_ref[...] + w[:, 0:1] * buf_ref[slot, 0] + w[:, 1:2] * buf_ref[slot, 1]


def _combine(pos_flat, y, x2, route_w):
    n, d = x2.shape
    tc = GATHER_TILE
    return pl.pallas_call(
        functools.partial(_combine_body, tc=tc, n_steps=n // tc),
        grid_spec=pltpu.PrefetchScalarGridSpec(
            num_scalar_prefetch=1,
            grid=(n // tc,),
            in_specs=[pl.BlockSpec(memory_space=pl.ANY),
                      pl.BlockSpec((tc, d), lambda i, pos: (i, 0)),
                      pl.BlockSpec((tc, LANES), lambda i, pos: (i, 0))],
            out_specs=pl.BlockSpec((tc, d), lambda i, pos: (i, 0)),
            scratch_shapes=[pltpu.VMEM((2, TOP_K, tc, d), F32), pltpu.SemaphoreType.DMA((2,))]),
        out_shape=jax.ShapeDtypeStruct((n, d), F32),
        compiler_params=pltpu.CompilerParams(
            dimension_semantics=("arbitrary",),
            vmem_limit_bytes=_vmem_limit([2 * _nbytes((tc, d), F32), _nbytes((tc, LANES), F32)],
                                         2 * TOP_K * _nbytes((tc, d), F32))),
        name="moe_combine_gather",
    )(pos_flat, y, x2, route_w)


def _routing_tables(route_i, counts, n_tokens, rows):
    tm = MOE_TILE
    expert = route_i[:, 0:TOP_K]
    rank = route_i[:, TOP_K:2 * TOP_K]
    cnt = counts[0, :N_EXPERTS].astype(jnp.int32)
    tiles_e = (cnt + (tm - 1)) // tm
    tile_end = jnp.cumsum(tiles_e)
    row_off = (tile_end - tiles_e) * tm
    pos = row_off[expert] + rank
    token = jnp.broadcast_to(jnp.arange(n_tokens, dtype=jnp.int32)[:, None], pos.shape)
    row_token = jnp.zeros((rows,), jnp.int32).at[pos.reshape(-1)].set(token.reshape(-1))
    tiles = jnp.arange(rows // tm, dtype=jnp.int32)
    tile_expert = jnp.minimum(jnp.sum(tiles[:, None] >= tile_end[None, :], axis=1),
                              N_EXPERTS - 1).astype(jnp.int32)
    tile_start = tile_end - tiles_e
    tile_rows = jnp.clip(cnt[tile_expert] - (tiles - tile_start[tile_expert]) * tm, 0, tm)
    n_used = tile_end[N_EXPERTS - 1:N_EXPERTS].astype(jnp.int32)
    experts = jnp.arange(N_EXPERTS, dtype=jnp.int32)
    nonempty = tiles_e > 0
    later = jnp.where(nonempty[None, :] & (experts[None, :] > experts[:, None]), experts[None, :], N_EXPERTS)
    first_group = jnp.min(jnp.where(nonempty, experts, N_EXPERTS))
    next_e = jnp.min(later, axis=1)
    next_e = jnp.where(next_e == N_EXPERTS, first_group, next_e)
    group_of_e = jnp.cumsum(nonempty.astype(jnp.int32)) - 1
    tile_first = (tiles == tile_start[tile_expert]).astype(jnp.int32)
    n_groups = jnp.sum(nonempty.astype(jnp.int32)).reshape(1)
    i32 = lambda v: v.astype(jnp.int32)
    tables = (tile_expert, i32(tile_rows), n_used, tile_first, i32(group_of_e[tile_expert]),
              i32(next_e[tile_expert]), i32(n_groups))
    return pos.reshape(-1).astype(jnp.int32), row_token, tables


def _moe_ffn(x2, gain, l, router_w, router_b, wg, wu, wd, m):
    n, d = x2.shape
    rows = n * TOP_K + N_EXPERTS * MOE_TILE
    rw = jnp.pad(router_w[m], ((0, 0), (0, LANES - N_EXPERTS)))
    rb = jnp.pad(router_b[m], (0, LANES - N_EXPERTS)).reshape(1, LANES)
    route_i, route_w, counts = _router(x2, gain, rw, rb, l)
    pos_flat, row_token, tables = _routing_tables(route_i, counts, n, rows)
    n_used = tables[2]
    xs = _dispatch(row_token, n_used * (MOE_TILE // GATHER_TILE), x2, gain, l, rows)
    act = _moe_up(tables, xs, wg, wu, m)
    y = _moe_down(tables, act, wd, m)
    return _combine(pos_flat, y, x2, route_w)


def _dense_ffn(x2, h, wg, wu, wd, m):
    n, d = x2.shape
    dff = wg.shape[-1]
    bm, bn = 1024, 512
    act = _mm("ffn_up_swiglu", n, dff, bm, bn,
              [(h, _a_spec(bm, d))],
              [(wg, _w_spec(d, bn, m)), (wu, _w_spec(d, bn, m))], [],
              [(0, 0), (0, 1)], _ep_swiglu, BF16)
    bm, bn = 512, 512
    return _mm("ffn_down_residual", n, d, bm, bn,
               [(act, _a_spec(bm, dff))],
               [(wd, _w_spec(dff, bn, m))],
               [(x2, pl.BlockSpec((bm, bn), lambda j, i: (i, j)))],
               [(0, 0)], _ep_residual, F32)


def _mixer(x2, l, batch, seq, norm_mix_g, w_in, lru_conv_w, lru_conv_b, lru_wa, lru_ba, lru_wx, lru_bx,
           lru_lambda, fox_bf, q_norm_g, k_norm_g, sc_conv_w, w_branch_lru, w_branch_att, w_branch_conv,
           w_merge, b_merge, w_out, ffn_gain):
    n, d = x2.shape
    depth = w_in.shape[0]
    u = _rmsnorm(x2, norm_mix_g, l)

    w_l = w_in[l]
    w_main = w_l[:, :COL_MAIN].astype(BF16)[None]
    w_tail = w_l[:, COL_TAIL:].astype(BF16)[None]
    bm, bn = 1024, 1024
    z = _mm("in_proj_main", n, COL_MAIN, bm, bn, [(u, _a_spec(bm, d))], [(w_main, _w_spec(d, bn, 0))], [],
            [(0, 0)], _ep_identity, F32)
    zt = _mm("in_proj_tail", n, 3 * CONV_WIDTH, bm, bn, [(u, _a_spec(bm, d))],
             [(w_tail, _w_spec(d, bn, 0))], [], [(0, 0)], _ep_identity, F32)
    z3 = z.reshape(batch, seq, COL_MAIN)
    zt3 = zt.reshape(batch, seq, 3 * CONV_WIDTH)

    wf = jnp.pad(w_l[:, COL_F:COL_TAIL], ((0, 0), (0, LANES - ATT_HEADS))).astype(BF16)
    bf = jnp.pad(fox_bf[l], (0, LANES - ATT_HEADS)).reshape(1, LANES)
    cum = _fgate(u, wf, bf, batch, seq)[:, :ATT_HEADS].reshape(batch, seq, ATT_HEADS)
    cum_h = cum.transpose(0, 2, 1)
    cum_col = cum_h.reshape(batch, ATT_HEADS, seq, 1)
    cum_row = cum_h.reshape(batch, ATT_HEADS, 1, seq)

    y_lru = _lru_branch(z3, lru_conv_w, lru_conv_b, lru_wa, lru_ba, lru_wx, lru_bx, lru_lambda, l)
    y_att = _attention(z3, cum_col, cum_row, q_norm_g, k_norm_g, l)
    y_conv = _sconv_branch(zt3, sc_conv_w, l)

    bm, bn = 1024, 256
    nb = d // bn
    ws = _w_spec
    merged = _mm(
        "gated_merge", n, d, bm, bn,
        [(u, _a_spec(bm, d)), (y_lru.reshape(n, LRU_WIDTH), _a_spec(bm, LRU_WIDTH)),
         (y_att.reshape(n, ATT_WIDTH), _a_spec(bm, ATT_WIDTH)),
         (y_conv.reshape(n, CONV_WIDTH), _a_spec(bm, CONV_WIDTH))],
        [(w_merge, ws(d, bn, l, 0)), (w_merge, ws(d, bn, l, nb)), (w_merge, ws(d, bn, l, 2 * nb)),
         (w_branch_lru, ws(LRU_WIDTH, bn, l)), (w_branch_att, ws(ATT_WIDTH, bn, l)),
         (w_branch_conv, ws(CONV_WIDTH, bn, l))],
        [(b_merge.reshape(depth, 1, -1), _row_spec(bn, l, g * nb)) for g in range(3)],
        [(0, 0), (0, 1), (0, 2), (1, 3), (2, 4), (3, 5)], _ep_merge, BF16)

    return _out_proj(merged, w_out, x2, l, ffn_gain)


def _out_proj_body(a_ref, w_ref, x_ref, *rest):
    xn = x_ref[...] + jnp.dot(a_ref[...], w_ref[...].astype(BF16), preferred_element_type=F32)
    if len(rest) == 1:
        rest[0][...] = xn
    else:
        g_ref, o_ref, h_ref = rest
        o_ref[...] = xn
        h = xn * lax.rsqrt(jnp.mean(xn * xn, axis=-1, keepdims=True) + EPS) * g_ref[...]
        h_ref[...] = h.astype(h_ref.dtype)


def _out_proj(merged, w_out, x2, l, ffn_gain):
    n, d = x2.shape
    bm = 256
    row = pl.BlockSpec((bm, d), lambda i: (i, 0))
    in_specs = [row, pl.BlockSpec((None, d, d), lambda i: (l, 0, 0), pipeline_mode=pl.Buffered(1)), row]
    args = [merged, w_out, x2]
    out_specs, out_shape = row, jax.ShapeDtypeStruct((n, d), F32)
    if ffn_gain is not None:
        in_specs.append(pl.BlockSpec((None, 1, d), lambda i: (l, 0, 0)))
        args.append(ffn_gain.reshape(ffn_gain.shape[0], 1, d))
        out_specs = [row, row]
        out_shape = [out_shape, jax.ShapeDtypeStruct((n, d), BF16)]
    blocks = [_nbytes((bm, d), BF16), 2 * _nbytes((bm, d), F32), _nbytes((bm, d), BF16)]
    return pl.pallas_call(
        _out_proj_body,
        grid=(n // bm,),
        in_specs=in_specs,
        out_specs=out_specs,
        out_shape=out_shape,
        compiler_params=pltpu.CompilerParams(
            dimension_semantics=("arbitrary",),
            vmem_limit_bytes=_vmem_limit(blocks, _nbytes((d, d), BF16) + 2 * _nbytes((bm, d), F32),
                                         _nbytes((d, d), F32))),
        name="out_proj_residual",
    )(*args)


def kernel(x, norm_mix_g, w_in, lru_conv_w, lru_conv_b, lru_wa, lru_ba, lru_wx, lru_bx, lru_lambda, fox_bf, q_norm_g, k_norm_g, sc_conv_w, w_branch_lru, w_branch_att, w_branch_conv, w_merge, b_merge, w_out, norm_ffn_g, ffn_wg, ffn_wu, ffn_wd, router_w, router_b, moe_wg, moe_wu, moe_wd):
    batch, seq, d = x.shape
    depth = w_in.shape[0]
    x2 = x.reshape(batch * seq, d)
    for l in range(depth):
        dense = l % 2 == 0
        mixed = _mixer(x2, l, batch, seq, norm_mix_g, w_in, lru_conv_w, lru_conv_b, lru_wa, lru_ba, lru_wx,
                       lru_bx, lru_lambda, fox_bf, q_norm_g, k_norm_g, sc_conv_w, w_branch_lru, w_branch_att,
                       w_branch_conv, w_merge, b_merge, w_out, norm_ffn_g if dense else None)
        if dense:
            x2, h = mixed
            x2 = _dense_ffn(x2, h, ffn_wg, ffn_wu, ffn_wd, l // 2)
        else:
            x2 = _moe_ffn(mixed, norm_ffn_g, l, router_w, router_b, moe_wg, moe_wu, moe_wd, l // 2)
    return x2.reshape(batch, seq, d)
```

```python
import functools

import jax
import jax.numpy as jnp
from jax import lax
from jax.experimental import pallas as pl
from jax.experimental.pallas import tpu as pltpu

F32 = jnp.float32
BF16 = jnp.bfloat16

D_MODEL = 2048
LRU_WIDTH = 1024
LRU_BLOCKS = 8
LRU_BLOCK = 128
LRU_CONV = 4
LRU_C = 8.0
ATT_HEADS = 8
ATT_HEAD_DIM = 128
ATT_WIDTH = 1024
CONV_WIDTH = 1024
CONV_K = 3
N_EXPERTS = 8
TOP_K = 2
EPS = 1e-6
COL_MAIN = 2 * LRU_WIDTH + 3 * ATT_WIDTH
COL_F = COL_MAIN
COL_TAIL = COL_MAIN + ATT_HEADS

LANES = 128
SUBLANES = 8
V7X_VMEM_BUDGET = 56 * 1024 * 1024

ROW_TILE = 512
SEQ_TILE = 256
Q_TILE = 256
MOE_TILE = 512
GATHER_TILE = 512
GATHER_UNROLL = 8


def _vmem_limit(block_bytes, temp_bytes=0, single_bytes=0):
    need = 2 * sum(block_bytes) + single_bytes + temp_bytes + (4 << 20)
    return int(min(max(need, 16 << 20), V7X_VMEM_BUDGET))


def _nbytes(shape, dtype):
    n = 1
    for s in shape:
        n *= s
    return n * jnp.dtype(dtype).itemsize


def _softplus(y):
    return jnp.maximum(y, 0.0) + jnp.log1p(jnp.exp(-jnp.abs(y)))


def _sigmoid(y):
    return 1.0 / (1.0 + jnp.exp(-y))


def _rmsnorm_body(x_ref, g_ref, o_ref):
    x = x_ref[...]
    y = x * lax.rsqrt(jnp.mean(x * x, axis=-1, keepdims=True) + EPS)
    o_ref[...] = (y * g_ref[...]).astype(o_ref.dtype)


def _rmsnorm(x2, gain, l):
    n, d = x2.shape
    return pl.pallas_call(
        _rmsnorm_body,
        grid=(n // ROW_TILE,),
        in_specs=[pl.BlockSpec((ROW_TILE, d), lambda i: (i, 0)),
                  pl.BlockSpec((None, 1, d), lambda i: (l, 0, 0))],
        out_specs=pl.BlockSpec((ROW_TILE, d), lambda i: (i, 0)),
        out_shape=jax.ShapeDtypeStruct((n, d), BF16),
        compiler_params=pltpu.CompilerParams(dimension_semantics=("arbitrary",)),
        name="rmsnorm",
    )(x2, gain.reshape(gain.shape[0], 1, d))


def _mm_body(*refs, pairs, n_a, n_w, n_e, epilogue, w_rows_are_outputs):
    a_refs = refs[:n_a]
    w_refs = refs[n_a:n_a + n_w]
    e_refs = refs[n_a + n_w:n_a + n_w + n_e]
    o_ref = refs[-1]
    dims = (((1,), (1,)), ((), ())) if w_rows_are_outputs else (((1,), (0,)), ((), ()))
    accs = [lax.dot_general(a_refs[ai][...], w_refs[wi][...].astype(BF16), dims, preferred_element_type=F32)
            for ai, wi in pairs]
    o_ref[...] = epilogue(accs, [e[...] for e in e_refs]).astype(o_ref.dtype)


def _mm(name, m, n, bm, bn, a_ops, w_ops, e_ops, pairs, epilogue, out_dtype, w_rows_are_outputs=False):
    arrays = [a for a, _ in a_ops] + [w for w, _ in w_ops] + [e for e, _ in e_ops]
    specs = [s for _, s in a_ops] + [s for _, s in w_ops] + [s for _, s in e_ops]
    blocks, single = [], 0
    for arr, spec in a_ops + w_ops + e_ops:
        nb = _nbytes([s for s in spec.block_shape if s is not None], arr.dtype)
        if spec.pipeline_mode is not None and spec.pipeline_mode.buffer_count == 1:
            single += nb
        else:
            blocks.append(nb)
    blocks.append(_nbytes((bm, bn), out_dtype))
    w_bf16 = sum(_nbytes([s for s in spec.block_shape if s is not None], BF16)
                 for w, spec in w_ops if w.dtype != BF16)
    temps = w_bf16 + len(pairs) * _nbytes((bm, bn), F32)
    body = functools.partial(_mm_body, pairs=tuple(pairs), n_a=len(a_ops), n_w=len(w_ops),
                             n_e=len(e_ops), epilogue=epilogue, w_rows_are_outputs=w_rows_are_outputs)
    return pl.pallas_call(
        body,
        grid=(pl.cdiv(n, bn), m // bm),
        in_specs=specs,
        out_specs=pl.BlockSpec((bm, bn), lambda j, i: (i, j)),
        out_shape=jax.ShapeDtypeStruct((m, n), out_dtype),
        compiler_params=pltpu.CompilerParams(
            dimension_semantics=("arbitrary", "arbitrary"),
            vmem_limit_bytes=_vmem_limit(blocks, temps, single)),
        name=name,
    )(*arrays)


def _a_spec(bm, k):
    return pl.BlockSpec((bm, k), lambda j, i: (i, 0))


def _w_spec(k, bn, l, col_block0=0, single_buffer=False):
    mode = pl.Buffered(1) if single_buffer else None
    return pl.BlockSpec((None, k, bn), lambda j, i: (l, 0, j + col_block0), pipeline_mode=mode)


def _row_spec(bn, l, col_block0=0):
    return pl.BlockSpec((None, 1, bn), lambda j, i: (l, 0, j + col_block0))


def _ep_identity(accs, extras):
    return accs[0]


def _ep_residual(accs, extras):
    return extras[0] + accs[0]


def _ep_swiglu(accs, extras):
    g, u = accs
    return (g * _sigmoid(g)) * u


def _ep_merge(accs, extras):
    g0, g1, g2, p0, p1, p2 = accs
    b0, b1, b2 = extras
    return _sigmoid(g0 + b0) * p0 + _sigmoid(g1 + b1) * p1 + _sigmoid(g2 + b2) * p2


def _fgate_body(u_ref, wf_ref, bf_ref, cum_ref, carry_ref, *, ts):
    @pl.when(pl.program_id(1) == 0)
    def _():
        carry_ref[...] = jnp.zeros_like(carry_ref)

    f = lax.dot_general(u_ref[...], wf_ref[...].astype(BF16), (((1,), (1,)), ((), ())),
                        preferred_element_type=F32) + bf_ref[...]
    lf = jnp.minimum(f, 0.0) - jnp.log1p(jnp.exp(-jnp.abs(f)))
    row = lax.broadcasted_iota(jnp.int32, (ts, ts), 0)
    col = lax.broadcasted_iota(jnp.int32, (ts, ts), 1)
    tri = (row >= col).astype(BF16)
    hi = lf.astype(BF16)
    r1 = lf - hi.astype(F32)
    mid = r1.astype(BF16)
    lo = (r1 - mid.astype(F32)).astype(BF16)
    cum = (jnp.dot(tri, hi, preferred_element_type=F32)
           + jnp.dot(tri, mid, preferred_element_type=F32)
           + jnp.dot(tri, lo, preferred_element_type=F32)) + carry_ref[0:1, :]
    cum_ref[...] = cum
    carry_ref[...] = jnp.broadcast_to(cum[ts - 1:ts, :], carry_ref.shape)


def _fgate(u2, wf, bf, batch, seq):
    ts = ROW_TILE
    nsb = seq // ts
    return pl.pallas_call(
        functools.partial(_fgate_body, ts=ts),
        grid=(batch, nsb),
        in_specs=[pl.BlockSpec((ts, D_MODEL), lambda b, s: (b * nsb + s, 0)),
                  pl.BlockSpec((LANES, D_MODEL), lambda b, s: (0, 0)),
                  pl.BlockSpec((1, LANES), lambda b, s: (0, 0))],
        out_specs=pl.BlockSpec((ts, LANES), lambda b, s: (b * nsb + s, 0)),
        out_shape=jax.ShapeDtypeStruct((batch * seq, LANES), F32),
        scratch_shapes=[pltpu.VMEM((SUBLANES, LANES), F32)],
        compiler_params=pltpu.CompilerParams(dimension_semantics=("arbitrary", "arbitrary")),
        name="fgate_cumsum",
    )(u2, wf, bf)


def _shifted(ext_ref, k, ts):
    return ext_ref[pl.ds(SUBLANES - k, ts), :]


def _lru_body(x_ref, g_ref, cw_ref, cb_ref, wa_ref, ba_ref, wx_ref, bx_ref, lam_ref, y_ref,
              ext_ref, h_ref, *, ts):
    @pl.when(pl.program_id(1) == 0)
    def _():
        ext_ref[0:SUBLANES, :] = jnp.zeros((SUBLANES, LRU_WIDTH), F32)
        h_ref[...] = jnp.zeros_like(h_ref)

    x = x_ref[...]
    ext_ref[pl.ds(SUBLANES, ts), :] = x
    cw = cw_ref[...]
    xc = x * cw[LRU_CONV - 1:LRU_CONV, :] + cb_ref[...]
    for k in range(1, LRU_CONV):
        xc = xc + _shifted(ext_ref, k, ts) * cw[LRU_CONV - 1 - k:LRU_CONV - k, :]
    ext_ref[0:SUBLANES, :] = x[ts - SUBLANES:ts, :]

    xcb = xc.astype(BF16)
    ra, ia = [], []
    for n in range(LRU_BLOCKS):
        blk = xcb[:, n * LRU_BLOCK:(n + 1) * LRU_BLOCK]
        ra.append(jnp.dot(blk, wa_ref[n].astype(BF16), preferred_element_type=F32))
        ia.append(jnp.dot(blk, wx_ref[n].astype(BF16), preferred_element_type=F32))
    r = _sigmoid(jnp.concatenate(ra, axis=1) + ba_ref[...])
    i = _sigmoid(jnp.concatenate(ia, axis=1) + bx_ref[...])
    log_a = (-LRU_C) * r * _softplus(-lam_ref[...])
    a = jnp.exp(log_a)
    th = jnp.tanh(log_a)
    one_m_a2 = -2.0 * th / (1.0 - th)
    mult = jnp.where(one_m_a2 > 0.0, one_m_a2 * lax.rsqrt(one_m_a2), 0.0)
    b = mult * (i * xc)

    rowm = lax.broadcasted_iota(jnp.int32, (SUBLANES, LRU_WIDTH), 0)
    h = h_ref[0:1, :]
    outs = []
    for c in range(ts // SUBLANES):
        sl = slice(c * SUBLANES, (c + 1) * SUBLANES)
        ac, bc = a[sl, :], b[sl, :]
        for d in (1, 2, 4):
            m = rowm >= d
            bc = jnp.where(m, ac * pltpu.roll(bc, d, 0) + bc, bc)
            ac = jnp.where(m, ac * pltpu.roll(ac, d, 0), ac)
        hc = ac * h + bc
        outs.append(hc)
        h = hc[SUBLANES - 1:SUBLANES, :]
    h_ref[...] = jnp.broadcast_to(h, h_ref.shape)
    hs = jnp.concatenate(outs, axis=0)
    y_ref[...] = (hs * jax.nn.gelu(g_ref[...])).astype(y_ref.dtype)


def _lru_branch(z3, conv_w, conv_b, w_a, b_a, w_x, b_x, lam, l):
    batch, seq, _ = z3.shape
    ts = SEQ_TILE
    depth = conv_w.shape[0]
    vec = lambda p: p.reshape(depth, 1, LRU_WIDTH)
    vspec = pl.BlockSpec((None, 1, LRU_WIDTH), lambda b, s: (l, 0, 0))
    wspec = pl.BlockSpec((None, LRU_BLOCKS, LRU_BLOCK, LRU_BLOCK), lambda b, s: (l, 0, 0, 0))
    return pl.pallas_call(
        functools.partial(_lru_body, ts=ts),
        grid=(batch, seq // ts),
        in_specs=[pl.BlockSpec((None, ts, LRU_WIDTH), lambda b, s: (b, s, 0)),
                  pl.BlockSpec((None, ts, LRU_WIDTH), lambda b, s: (b, s, 1)),
                  pl.BlockSpec((None, LRU_CONV, LRU_WIDTH), lambda b, s: (l, 0, 0)),
                  vspec, wspec, vspec, wspec, vspec, vspec],
        out_specs=pl.BlockSpec((None, ts, LRU_WIDTH), lambda b, s: (b, s, 0)),
        out_shape=jax.ShapeDtypeStruct((batch, seq, LRU_WIDTH), BF16),
        scratch_shapes=[pltpu.VMEM((ts + SUBLANES, LRU_WIDTH), F32),
                        pltpu.VMEM((SUBLANES, LRU_WIDTH), F32)],
        compiler_params=pltpu.CompilerParams(dimension_semantics=("arbitrary", "arbitrary")),
        name="rglru_branch",
    )(z3, z3, conv_w, vec(conv_b), w_a, vec(b_a), w_x, vec(b_x), vec(lam))


def _sconv_body(b_ref, c_ref, h_ref, w_ref, y_ref, ext_ref, *, ts):
    @pl.when(pl.program_id(1) == 0)
    def _():
        ext_ref[0:SUBLANES, :] = jnp.zeros((SUBLANES, CONV_WIDTH), F32)

    ch = c_ref[...] * h_ref[...]
    ext_ref[pl.ds(SUBLANES, ts), :] = ch
    w = w_ref[...]
    y = ch * w[CONV_K - 1:CONV_K, :]
    for k in range(1, CONV_K):
        y = y + _shifted(ext_ref, k, ts) * w[CONV_K - 1 - k:CONV_K - k, :]
    ext_ref[0:SUBLANES, :] = ch[ts - SUBLANES:ts, :]
    y_ref[...] = (b_ref[...] * y).astype(y_ref.dtype)


def _sconv_branch(zt3, conv_w, l):
    batch, seq, _ = zt3.shape
    ts = SEQ_TILE
    col = lambda c: pl.BlockSpec((None, ts, CONV_WIDTH), lambda b, s: (b, s, c))
    return pl.pallas_call(
        functools.partial(_sconv_body, ts=ts),
        grid=(batch, seq // ts),
        in_specs=[col(0), col(1), col(2),
                  pl.BlockSpec((None, CONV_K, CONV_WIDTH), lambda b, s: (l, 0, 0))],
        out_specs=pl.BlockSpec((None, ts, CONV_WIDTH), lambda b, s: (b, s, 0)),
        out_shape=jax.ShapeDtypeStruct((batch, seq, CONV_WIDTH), BF16),
        scratch_shapes=[pltpu.VMEM((ts + SUBLANES, CONV_WIDTH), F32)],
        compiler_params=pltpu.CompilerParams(dimension_semantics=("arbitrary", "arbitrary")),
        name="short_conv_branch",
    )(zt3, zt3, zt3, conv_w)


def _head_norm(t, gain):
    return t * lax.rsqrt(jnp.mean(t * t, axis=-1, keepdims=True) + EPS) * gain


def _attn_body(q_ref, k_ref, v_ref, cc_ref, cr_ref, qg_ref, kg_ref, o_ref, *, seq, tq):
    log2e = 1.4426950408889634
    qn = (_head_norm(q_ref[...], qg_ref[...]) * (ATT_HEAD_DIM ** -0.5 * log2e)).astype(BF16)
    kn = _head_norm(k_ref[...], kg_ref[...]).astype(BF16)
    v = v_ref[...].astype(BF16)
    cum_k = cr_ref[...] * log2e
    causal = (lax.broadcasted_iota(jnp.int32, (tq, tq), 0) >= lax.broadcasted_iota(jnp.int32, (tq, tq), 1))
    nt = (((1,), (1,)), ((), ()))
    def scores(qi):
        q0, kv = qi * tq, (qi + 1) * tq
        return lax.dot_general(qn[q0:kv, :], kn[:kv, :], nt, preferred_element_type=F32) - cum_k[:, :kv]

    n_q = seq // tq
    t_next = scores(0)
    for qi in range(n_q):
        q0, kv = qi * tq, (qi + 1) * tq
        cum_q = cc_ref[q0:kv, :] * log2e
        t = t_next
        if qi + 1 < n_q:
            t_next = scores(qi + 1)
        td = jnp.where(causal, t[:, q0:kv], -jnp.inf)
        t = td if qi == 0 else jnp.concatenate([t[:, :q0], td], axis=1)
        mt = jnp.max(t, axis=-1, keepdims=True)
        off = cum_q - (mt + cum_q)
        p = jnp.exp2(t + off)
        denom = jnp.sum(p, axis=-1, keepdims=True)
        o = jnp.dot(p.astype(BF16), v[:kv, :], preferred_element_type=F32)
        o_ref[q0:kv, :] = (o / denom).astype(o_ref.dtype)


def _attention(z3, cum_col, cum_row, q_gain, k_gain, l):
    batch, seq, _ = z3.shape
    depth = q_gain.shape[0]
    hd = ATT_HEAD_DIM
    head = lambda base: pl.BlockSpec((None, seq, hd), lambda b, h: (b, 0, base + h))
    gspec = pl.BlockSpec((None, 1, hd), lambda b, h: (l, 0, 0))
    q_base = 2 * LRU_WIDTH // hd
    return pl.pallas_call(
        functools.partial(_attn_body, seq=seq, tq=Q_TILE),
        grid=(batch, ATT_HEADS),
        in_specs=[head(q_base), head(q_base + ATT_HEADS), head(q_base + 2 * ATT_HEADS),
                  pl.BlockSpec((None, None, seq, 1), lambda b, h: (b, h, 0, 0)),
                  pl.BlockSpec((None, None, 1, seq), lambda b, h: (b, h, 0, 0)),
                  gspec, gspec],
        out_specs=pl.BlockSpec((None, seq, hd), lambda b, h: (b, 0, h)),
        out_shape=jax.ShapeDtypeStruct((batch, seq, ATT_WIDTH), BF16),
        compiler_params=pltpu.CompilerParams(
            dimension_semantics=("arbitrary", "arbitrary"),
            vmem_limit_bytes=_vmem_limit([3 * _nbytes((seq, hd), F32), _nbytes((seq, LANES), F32),
                                          _nbytes((SUBLANES, seq), F32), _nbytes((seq, hd), BF16)],
                                         8 * _nbytes((Q_TILE, seq), F32))),
        name="forgetting_attention",
    )(z3, z3, z3, cum_col, cum_row, q_gain.reshape(depth, 1, hd), k_gain.reshape(depth, 1, hd))


def _router_body(x_ref, g_ref, rw_ref, rb_ref, ri_ref, rwt_ref, cnt_ref, carry_ref, *, bm):
    @pl.when(pl.program_id(0) == 0)
    def _():
        carry_ref[...] = jnp.zeros_like(carry_ref)

    x = x_ref[...]
    h = x * lax.rsqrt(jnp.mean(x * x, axis=-1, keepdims=True) + EPS) * g_ref[...]
    logits = jnp.dot(h, rw_ref[...], preferred_element_type=F32,
                     precision=lax.Precision.HIGHEST) + rb_ref[...]
    lane = lax.broadcasted_iota(jnp.int32, (bm, LANES), 1)
    logits = jnp.where(lane < N_EXPERTS, logits, -jnp.inf)
    m1 = jnp.max(logits, axis=-1, keepdims=True)
    i1 = jnp.min(jnp.where(logits == m1, lane, LANES), axis=-1, keepdims=True)
    rest = jnp.where(lane == i1, -jnp.inf, logits)
    m2 = jnp.max(rest, axis=-1, keepdims=True)
    i2 = jnp.min(jnp.where(rest == m2, lane, LANES), axis=-1, keepdims=True)
    e2 = jnp.exp(m2 - m1)
    w1 = 1.0 / (1.0 + e2)
    w2 = e2 / (1.0 + e2)

    onehot = ((lane == i1) | (lane == i2)).astype(BF16)
    row = lax.broadcasted_iota(jnp.int32, (bm, bm), 0)
    col = lax.broadcasted_iota(jnp.int32, (bm, bm), 1)
    tri = (row > col).astype(BF16)
    before = jnp.dot(tri, onehot, preferred_element_type=F32) + carry_ref[0:1, :]
    rank1 = jnp.sum(jnp.where(lane == i1, before, 0.0), axis=-1, keepdims=True).astype(jnp.int32)
    rank2 = jnp.sum(jnp.where(lane == i2, before, 0.0), axis=-1, keepdims=True).astype(jnp.int32)
    total = carry_ref[0:1, :] + jnp.sum(onehot.astype(F32), axis=0, keepdims=True)
    carry_ref[...] = jnp.broadcast_to(total, carry_ref.shape)
    cnt_ref[...] = jnp.broadcast_to(total, cnt_ref.shape)

    ri_ref[...] = jnp.where(lane == 0, i1, jnp.where(lane == 1, i2,
                            jnp.where(lane == 2, rank1, jnp.where(lane == 3, rank2, 0))))
    rwt_ref[...] = jnp.where(lane == 0, w1, jnp.where(lane == 1, w2, 0.0))


def _router(x2, gain, rw, rb, l):
    n, d = x2.shape
    bm = ROW_TILE
    return pl.pallas_call(
        functools.partial(_router_body, bm=bm),
        grid=(n // bm,),
        in_specs=[pl.BlockSpec((bm, d), lambda i: (i, 0)),
                  pl.BlockSpec((None, 1, d), lambda i: (l, 0, 0)),
                  pl.BlockSpec((d, LANES), lambda i: (0, 0)),
                  pl.BlockSpec((1, LANES), lambda i: (0, 0))],
        out_specs=[pl.BlockSpec((bm, LANES), lambda i: (i, 0)),
                   pl.BlockSpec((bm, LANES), lambda i: (i, 0)),
                   pl.BlockSpec((SUBLANES, LANES), lambda i: (0, 0))],
        out_shape=[jax.ShapeDtypeStruct((n, LANES), jnp.int32),
                   jax.ShapeDtypeStruct((n, LANES), F32),
                   jax.ShapeDtypeStruct((SUBLANES, LANES), F32)],
        scratch_shapes=[pltpu.VMEM((SUBLANES, LANES), F32)],
        compiler_params=pltpu.CompilerParams(dimension_semantics=("arbitrary",)),
        name="moe_router",
    )(x2, gain.reshape(gain.shape[0], 1, d), rw, rb)


def _row_copy(src_hbm, dst_vmem, sem, src_row, dst_row):
    return pltpu.make_async_copy(src_hbm.at[pl.ds(src_row, 1), :], dst_vmem.at[pl.ds(dst_row, 1), :], sem)


def _dispatch_body(tok_ref, ng_ref, x_hbm, g_ref, o_ref, buf_ref, sem, *, tg):
    i = pl.program_id(0)
    slot = i & 1
    n_live = ng_ref[0]

    def issue_tile(step, slot_):
        def issue(pair, c):
            for p in range(2):
                r = 2 * pair + p
                _row_copy(x_hbm, buf_ref.at[slot_], sem.at[slot_], tok_ref[step * tg + r], r).start(priority=p)
            return c
        lax.fori_loop(0, tg // 2, issue, 0, unroll=GATHER_UNROLL // 2)

    @pl.when(i == 0)
    def _():
        issue_tile(0, 0)

    @pl.when(i + 1 < n_live)
    def _():
        issue_tile(i + 1, 1 - slot)

    @pl.when(i < n_live)
    def _():
        def drain(r, c):
            _row_copy(x_hbm, buf_ref.at[slot], sem.at[slot], 0, r).wait()
            return c
        lax.fori_loop(0, tg, drain, 0, unroll=GATHER_UNROLL)
        x = buf_ref[slot]
        h = x * lax.rsqrt(jnp.mean(x * x, axis=-1, keepdims=True) + EPS) * g_ref[...]
        o_ref[...] = h.astype(o_ref.dtype)

    @pl.when(i >= n_live)
    def _():
        o_ref[...] = jnp.zeros_like(o_ref)


def _dispatch(row_token, n_gather_tiles, x2, gain, l, rows):
    n, d = x2.shape
    tg = GATHER_TILE
    return pl.pallas_call(
        functools.partial(_dispatch_body, tg=tg),
        grid_spec=pltpu.PrefetchScalarGridSpec(
            num_scalar_prefetch=2,
            grid=(rows // tg,),
            in_specs=[pl.BlockSpec(memory_space=pl.ANY),
                      pl.BlockSpec((None, 1, d), lambda i, tok, ng: (l, 0, 0))],
            out_specs=pl.BlockSpec((tg, d), lambda i, tok, ng: (i, 0)),
            scratch_shapes=[pltpu.VMEM((2, tg, d), F32), pltpu.SemaphoreType.DMA((2,))]),
        out_shape=jax.ShapeDtypeStruct((rows, d), BF16),
        compiler_params=pltpu.CompilerParams(dimension_semantics=("arbitrary",)),
        name="moe_dispatch_gather",
    )(row_token, n_gather_tiles, x2, gain.reshape(gain.shape[0], 1, d))


N_MOE_TABLES = 7


def _weight_copies(w_hbms, wbuf, sem, m, expert, j, slot, bn):
    col = pl.multiple_of(j * bn, bn)
    return [pltpu.make_async_copy(w.at[m, expert, :, pl.ds(col, bn)], wbuf.at[slot, k], sem.at[slot, k])
            for k, w in enumerate(w_hbms)]


def _moe_step(tables, w_hbms, wbuf, sem, o_ref, compute, *, tm, bn, m, n_j):
    te_ref, tr_ref, nu_ref, fl_ref, gi_ref, ne_ref, ng_ref = tables
    j, t = pl.program_id(0), pl.program_id(1)
    used = t < nu_ref[0]
    half = tr_ref[t] <= tm // 2
    n_groups = ng_ref[0]
    gidx = j * n_groups + gi_ref[t]
    slot = gidx & 1

    @pl.when(used & (fl_ref[t] == 1))
    def _():
        @pl.when(gidx == 0)
        def _():
            for c in _weight_copies(w_hbms, wbuf, sem, m, te_ref[t], 0, 0, bn):
                c.start()
        for c in _weight_copies(w_hbms, wbuf, sem, m, te_ref[t], j, slot, bn):
            c.wait()
        j_next = j + jnp.where(gi_ref[t] == n_groups - 1, 1, 0)

        @pl.when(j_next < n_j)
        def _():
            for c in _weight_copies(w_hbms, wbuf, sem, m, ne_ref[t], j_next, 1 - slot, bn):
                c.start()

    def run(rows):
        compute(rows, slot)
        if rows < tm:
            o_ref[rows:tm, :] = jnp.zeros((tm - rows, o_ref.shape[1]), o_ref.dtype)

    pl.when(used & jnp.logical_not(half))(lambda: run(tm))
    pl.when(used & half)(lambda: run(tm // 2))

    @pl.when(jnp.logical_not(used))
    def _():
        o_ref[...] = jnp.zeros_like(o_ref)


def _moe_up_body(*refs, **cfg):
    tables = refs[:N_MOE_TABLES]
    a_ref, wg_hbm, wu_hbm, o_ref, wbuf, sem = refs[N_MOE_TABLES:]

    def compute(rows, slot):
        a = a_ref[0:rows, :]
        g = jnp.dot(a, wbuf[slot, 0].astype(BF16), preferred_element_type=F32)
        u = jnp.dot(a, wbuf[slot, 1].astype(BF16), preferred_element_type=F32)
        o_ref[0:rows, :] = ((g * _sigmoid(g)) * u).astype(o_ref.dtype)

    _moe_step(tables, (wg_hbm, wu_hbm), wbuf, sem, o_ref, compute, **cfg)


def _moe_down_body(*refs, **cfg):
    tables = refs[:N_MOE_TABLES]
    a_ref, wd_hbm, o_ref, wbuf, sem = refs[N_MOE_TABLES:]

    def compute(rows, slot):
        o_ref[0:rows, :] = jnp.dot(a_ref[0:rows, :], wbuf[slot, 0].astype(BF16), preferred_element_type=F32)

    _moe_step(tables, (wd_hbm,), wbuf, sem, o_ref, compute, **cfg)


def _tile_clamp(t, nu):
    return jnp.maximum(jnp.minimum(t, nu[0] - 1), 0)


def _moe_grouped(name, body, tables, a, weights, m, bn, out_dtype, n_acc):
    rows, k = a.shape
    n_out = weights[0].shape[-1]
    tm = MOE_TILE
    n_j = n_out // bn
    a_map = lambda j, t, te, tr, nu, fl, gi, ne, ng: (_tile_clamp(t, nu), 0)
    o_map = lambda j, t, te, tr, nu, fl, gi, ne, ng: (t, j)
    blocks = [_nbytes((tm, k), BF16), _nbytes((tm, bn), out_dtype)]
    wbuf_bytes = 2 * len(weights) * _nbytes((k, bn), F32)
    temps = len(weights) * _nbytes((k, bn), BF16) + n_acc * _nbytes((tm, bn), F32)
    return pl.pallas_call(
        functools.partial(body, tm=tm, bn=bn, m=m, n_j=n_j),
        grid_spec=pltpu.PrefetchScalarGridSpec(
            num_scalar_prefetch=N_MOE_TABLES,
            grid=(n_j, rows // tm),
            in_specs=[pl.BlockSpec((tm, k), a_map)] + [pl.BlockSpec(memory_space=pl.ANY)] * len(weights),
            out_specs=pl.BlockSpec((tm, bn), o_map),
            scratch_shapes=[pltpu.VMEM((2, len(weights), k, bn), F32),
                            pltpu.SemaphoreType.DMA((2, len(weights)))]),
        out_shape=jax.ShapeDtypeStruct((rows, n_out), out_dtype),
        compiler_params=pltpu.CompilerParams(dimension_semantics=("arbitrary", "arbitrary"),
                                             vmem_limit_bytes=_vmem_limit(blocks, temps, wbuf_bytes)),
        name=name,
    )(*tables, a, *weights)


def _moe_up(tables, xs, wg, wu, m):
    return _moe_grouped("moe_up_swiglu", _moe_up_body, tables, xs, (wg, wu), m, 1024, BF16, 3)


def _moe_down(tables, act, wd, m):
    return _moe_grouped("moe_down", _moe_down_body, tables, act, (wd,), m, 512, F32, 2)


def _combine_body(pos_ref, y_hbm, x_ref, w_ref, o_ref, buf_ref, sem, *, tc, n_steps):
    i = pl.program_id(0)
    slot = i & 1

    def issue_tile(step, slot_):
        def issue(r, c):
            for k in range(TOP_K):
                _row_copy(y_hbm, buf_ref.at[slot_, k], sem.at[slot_],
                          pos_ref[(step * tc + r) * TOP_K + k], r).start(priority=k)
            return c
        lax.fori_loop(0, tc, issue, 0, unroll=GATHER_UNROLL)

    @pl.when(i == 0)
    def _():
        issue_tile(0, 0)

    @pl.when(i + 1 < n_steps)
    def _():
        issue_tile(i + 1, 1 - slot)

    def drain(r, c):
        for k in range(TOP_K):
            _row_copy(y_hbm, buf_ref.at[slot, k], sem.at[slot], 0, r).wait()
        return c
    lax.fori_loop(0, tc, drain, 0, unroll=GATHER_UNROLL)
    w = w_ref[...]
    o_ref[...] = x_ref[...] + w[:, 0:1] * buf_ref[slot, 0] + w[:, 1:2] * buf_ref[slot, 1]


def _combine(pos_flat, y, x2, route_w):
    n, d = x2.shape
    tc = GATHER_TILE
    return pl.pallas_call(
        functools.partial(_combine_body, tc=tc, n_steps=n // tc),
        grid_spec=pltpu.PrefetchScalarGridSpec(
            num_scalar_prefetch=1,
            grid=(n // tc,),
            in_specs=[pl.BlockSpec(memory_space=pl.ANY),
                      pl.BlockSpec((tc, d), lambda i, pos: (i, 0)),
                      pl.BlockSpec((tc, LANES), lambda i, pos: (i, 0))],
            out_specs=pl.BlockSpec((tc, d), lambda i, pos: (i, 0)),
            scratch_shapes=[pltpu.VMEM((2, TOP_K, tc, d), F32), pltpu.SemaphoreType.DMA((2,))]),
        out_shape=jax.ShapeDtypeStruct((n, d), F32),
        compiler_params=pltpu.CompilerParams(
            dimension_semantics=("arbitrary",),
            vmem_limit_bytes=_vmem_limit([2 * _nbytes((tc, d), F32), _nbytes((tc, LANES), F32)],
                                         2 * TOP_K * _nbytes((tc, d), F32))),
        name="moe_combine_gather",
    )(pos_flat, y, x2, route_w)


def _routing_tables(route_i, counts, n_tokens, rows):
    tm = MOE_TILE
    expert = route_i[:, 0:TOP_K]
    rank = route_i[:, TOP_K:2 * TOP_K]
    cnt = counts[0, :N_EXPERTS].astype(jnp.int32)
    tiles_e = (cnt + (tm - 1)) // tm
    tile_end = jnp.cumsum(tiles_e)
    row_off = (tile_end - tiles_e) * tm
    pos = row_off[expert] + rank
    token = jnp.broadcast_to(jnp.arange(n_tokens, dtype=jnp.int32)[:, None], pos.shape)
    row_token = jnp.zeros((rows,), jnp.int32).at[pos.reshape(-1)].set(token.reshape(-1), unique_indices=True)
    tiles = jnp.arange(rows // tm, dtype=jnp.int32)
    tile_expert = jnp.minimum(jnp.sum(tiles[:, None] >= tile_end[None, :], axis=1),
                              N_EXPERTS - 1).astype(jnp.int32)
    tile_start = tile_end - tiles_e
    tile_rows = jnp.clip(cnt[tile_expert] - (tiles - tile_start[tile_expert]) * tm, 0, tm)
    n_used = tile_end[N_EXPERTS - 1:N_EXPERTS].astype(jnp.int32)
    experts = jnp.arange(N_EXPERTS, dtype=jnp.int32)
    nonempty = tiles_e > 0
    later = jnp.where(nonempty[None, :] & (experts[None, :] > experts[:, None]), experts[None, :], N_EXPERTS)
    first_group = jnp.min(jnp.where(nonempty, experts, N_EXPERTS))
    next_e = jnp.min(later, axis=1)
    next_e = jnp.where(next_e == N_EXPERTS, first_group, next_e)
    group_of_e = jnp.cumsum(nonempty.astype(jnp.int32)) - 1
    tile_first = (tiles == tile_start[tile_expert]).astype(jnp.int32)
    n_groups = jnp.sum(nonempty.astype(jnp.int32)).reshape(1)
    i32 = lambda v: v.astype(jnp.int32)
    tables = (tile_expert, i32(tile_rows), n_used, tile_first, i32(group_of_e[tile_expert]),
              i32(next_e[tile_expert]), i32(n_groups))
    return pos.reshape(-1).astype(jnp.int32), row_token, tables


def _moe_ffn(x2, gain, l, router_w, router_b, wg, wu, wd, m):
    n, d = x2.shape
    rows = n * TOP_K + N_EXPERTS * MOE_TILE
    rw = jnp.pad(router_w[m], ((0, 0), (0, LANES - N_EXPERTS)))
    rb = jnp.pad(router_b[m], (0, LANES - N_EXPERTS)).reshape(1, LANES)
    route_i, route_w, counts = _router(x2, gain, rw, rb, l)
    pos_flat, row_token, tables = _routing_tables(route_i, counts, n, rows)
    n_used = tables[2]
    xs = _dispatch(row_token, n_used * (MOE_TILE // GATHER_TILE), x2, gain, l, rows)
    act = _moe_up(tables, xs, wg, wu, m)
    y = _moe_down(tables, act, wd, m)
    return _combine(pos_flat, y, x2, route_w)


def _dense_ffn(x2, h, wg, wu, wd, m):
    n, d = x2.shape
    dff = wg.shape[-1]
    bm, bn = 1024, 512
    act = _mm("ffn_up_swiglu", n, dff, bm, bn,
              [(h, _a_spec(bm, d))],
              [(wg, _w_spec(d, bn, m)), (wu, _w_spec(d, bn, m))], [],
              [(0, 0), (0, 1)], _ep_swiglu, BF16)
    bm, bn = 512, 512
    return _mm("ffn_down_residual", n, d, bm, bn,
               [(act, _a_spec(bm, dff))],
               [(wd, _w_spec(dff, bn, m))],
               [(x2, pl.BlockSpec((bm, bn), lambda j, i: (i, j)))],
               [(0, 0)], _ep_residual, F32)


def _mixer(x2, l, batch, seq, norm_mix_g, w_in, lru_conv_w, lru_conv_b, lru_wa, lru_ba, lru_wx, lru_bx,
           lru_lambda, fox_bf, q_norm_g, k_norm_g, sc_conv_w, w_branch_lru, w_branch_att, w_branch_conv,
           w_merge, b_merge, w_out, ffn_gain):
    n, d = x2.shape
    depth = w_in.shape[0]
    u = _rmsnorm(x2, norm_mix_g, l)

    w_t = jnp.swapaxes(w_in, 1, 2)
    w_tail_t = w_t[l, COL_TAIL:, :][None]
    t_spec = lambda layer: pl.BlockSpec((None, bn, d), lambda j, i: (layer, j, 0))
    bm, bn = 1024, 1024
    z = _mm("in_proj_main", n, COL_MAIN, bm, bn, [(u, _a_spec(bm, d))], [(w_t, t_spec(l))], [],
            [(0, 0)], _ep_identity, F32, w_rows_are_outputs=True)
    zt = _mm("in_proj_tail", n, 3 * CONV_WIDTH, bm, bn, [(u, _a_spec(bm, d))],
             [(w_tail_t, t_spec(0))], [], [(0, 0)], _ep_identity, F32, w_rows_are_outputs=True)
    z3 = z.reshape(batch, seq, COL_MAIN)
    zt3 = zt.reshape(batch, seq, 3 * CONV_WIDTH)

    wf = jnp.pad(w_t[l, COL_F:COL_TAIL, :], ((0, LANES - ATT_HEADS), (0, 0)))
    bf = jnp.pad(fox_bf[l], (0, LANES - ATT_HEADS)).reshape(1, LANES)
    cum = _fgate(u, wf, bf, batch, seq)[:, :ATT_HEADS].reshape(batch, seq, ATT_HEADS)
    cum_h = cum.transpose(0, 2, 1)
    cum_col = cum_h.reshape(batch, ATT_HEADS, seq, 1)
    cum_row = cum_h.reshape(batch, ATT_HEADS, 1, seq)

    y_lru = _lru_branch(z3, lru_conv_w, lru_conv_b, lru_wa, lru_ba, lru_wx, lru_bx, lru_lambda, l)
    y_att = _attention(z3, cum_col, cum_row, q_norm_g, k_norm_g, l)
    y_conv = _sconv_branch(zt3, sc_conv_w, l)

    bm, bn = 1024, 256
    nb = d // bn
    ws = _w_spec
    merged = _mm(
        "gated_merge", n, d, bm, bn,
        [(u, _a_spec(bm, d)), (y_lru.reshape(n, LRU_WIDTH), _a_spec(bm, LRU_WIDTH)),
         (y_att.reshape(n, ATT_WIDTH), _a_spec(bm, ATT_WIDTH)),
         (y_conv.reshape(n, CONV_WIDTH), _a_spec(bm, CONV_WIDTH))],
        [(w_merge, ws(d, bn, l, 0)), (w_merge, ws(d, bn, l, nb)), (w_merge, ws(d, bn, l, 2 * nb)),
         (w_branch_lru, ws(LRU_WIDTH, bn, l)), (w_branch_att, ws(ATT_WIDTH, bn, l)),
         (w_branch_conv, ws(CONV_WIDTH, bn, l))],
        [(b_merge.reshape(depth, 1, -1), _row_spec(bn, l, g * nb)) for g in range(3)],
        [(0, 0), (0, 1), (0, 2), (1, 3), (2, 4), (3, 5)], _ep_merge, BF16)

    return _out_proj(merged, w_out, x2, l, ffn_gain)


def _out_proj_body(a_ref, w_ref, x_ref, *rest):
    xn = x_ref[...] + jnp.dot(a_ref[...], w_ref[...].astype(BF16), preferred_element_type=F32)
    if len(rest) == 1:
        rest[0][...] = xn
    else:
        g_ref, o_ref, h_ref = rest
        o_ref[...] = xn
        h = xn * lax.rsqrt(jnp.mean(xn * xn, axis=-1, keepdims=True) + EPS) * g_ref[...]
        h_ref[...] = h.astype(h_ref.dtype)


def _out_proj(merged, w_out, x2, l, ffn_gain):
    n, d = x2.shape
    bm = 256
    row = pl.BlockSpec((bm, d), lambda i: (i, 0))
    in_specs = [row, pl.BlockSpec((None, d, d), lambda i: (l, 0, 0), pipeline_mode=pl.Buffered(1)), row]
    args = [merged, w_out, x2]
    out_specs, out_shape = row, jax.ShapeDtypeStruct((n, d), F32)
    if ffn_gain is not None:
        in_specs.append(pl.BlockSpec((None, 1, d), lambda i: (l, 0, 0)))
        args.append(ffn_gain.reshape(ffn_gain.shape[0], 1, d))
        out_specs = [row, row]
        out_shape = [out_shape, jax.ShapeDtypeStruct((n, d), BF16)]
    blocks = [_nbytes((bm, d), BF16), 2 * _nbytes((bm, d), F32), _nbytes((bm, d), BF16)]
    return pl.pallas_call(
        _out_proj_body,
        grid=(n // bm,),
        in_specs=in_specs,
        out_specs=out_specs,
        out_shape=out_shape,
        compiler_params=pltpu.CompilerParams(
            dimension_semantics=("arbitrary",),
            vmem_limit_bytes=_vmem_limit(blocks, _nbytes((d, d), BF16) + 2 * _nbytes((bm, d), F32),
                                         _nbytes((d, d), F32))),
        name="out_proj_residual",
    )(*args)


def kernel(x, norm_mix_g, w_in, lru_conv_w, lru_conv_b, lru_wa, lru_ba, lru_wx, lru_bx, lru_lambda, fox_bf, q_norm_g, k_norm_g, sc_conv_w, w_branch_lru, w_branch_att, w_branch_conv, w_merge, b_merge, w_out, norm_ffn_g, ffn_wg, ffn_wu, ffn_wd, router_w, router_b, moe_wg, moe_wu, moe_wd):
    batch, seq, d = x.shape
    depth = w_in.shape[0]
    x2 = x.reshape(batch * seq, d)
    for l in range(depth):
        dense = l % 2 == 0
        mixed = _mixer(x2, l, batch, seq, norm_mix_g, w_in, lru_conv_w, lru_conv_b, lru_wa, lru_ba, lru_wx,
                       lru_bx, lru_lambda, fox_bf, q_norm_g, k_norm_g, sc_conv_w, w_branch_lru, w_branch_att,
                       w_branch_conv, w_merge, b_merge, w_out, norm_ffn_g if dense else None)
        if dense:
            x2, h = mixed
            x2 = _dense_ffn(x2, h, ffn_wg, ffn_wu, ffn_wd, l // 2)
        else:
            x2 = _moe_ffn(mixed, norm_ffn_g, l, router_w, router_b, moe_wg, moe_wu, moe_wd, l // 2)
    return x2.reshape(batch, seq, d)
```

```python
import functools

import jax
import jax.numpy as jnp
from jax import lax
from jax.experimental import pallas as pl
from jax.experimental.pallas import tpu as pltpu

F32 = jnp.float32
BF16 = jnp.bfloat16

D_MODEL = 2048
LRU_WIDTH = 1024
LRU_BLOCKS = 8
LRU_BLOCK = 128
LRU_CONV = 4
LRU_C = 8.0
ATT_HEADS = 8
ATT_HEAD_DIM = 128
ATT_WIDTH = 1024
CONV_WIDTH = 1024
CONV_K = 3
N_EXPERTS = 8
TOP_K = 2
EPS = 1e-6
COL_MAIN = 2 * LRU_WIDTH + 3 * ATT_WIDTH
COL_F = COL_MAIN
COL_TAIL = COL_MAIN + ATT_HEADS

LANES = 128
SUBLANES = 8
V7X_VMEM_BUDGET = 56 * 1024 * 1024

ROW_TILE = 512
SEQ_TILE = 256
Q_TILE = 256
MOE_TILE = 512
GATHER_TILE = 512


def _vmem_limit(block_bytes, temp_bytes=0, single_bytes=0):
    need = 2 * sum(block_bytes) + single_bytes + temp_bytes + (4 << 20)
    return int(min(max(need, 16 << 20), V7X_VMEM_BUDGET))


def _nbytes(shape, dtype):
    n = 1
    for s in shape:
        n *= s
    return n * jnp.dtype(dtype).itemsize


def _softplus(y):
    return jnp.maximum(y, 0.0) + jnp.log1p(jnp.exp(-jnp.abs(y)))


def _sigmoid(y):
    return 1.0 / (1.0 + jnp.exp(-y))


def _rmsnorm_body(x_ref, g_ref, o_ref):
    x = x_ref[...]
    y = x * lax.rsqrt(jnp.mean(x * x, axis=-1, keepdims=True) + EPS)
    o_ref[...] = (y * g_ref[...]).astype(o_ref.dtype)


def _rmsnorm(x2, gain, l):
    n, d = x2.shape
    return pl.pallas_call(
        _rmsnorm_body,
        grid=(n // ROW_TILE,),
        in_specs=[pl.BlockSpec((ROW_TILE, d), lambda i: (i, 0)),
                  pl.BlockSpec((None, 1, d), lambda i: (l, 0, 0))],
        out_specs=pl.BlockSpec((ROW_TILE, d), lambda i: (i, 0)),
        out_shape=jax.ShapeDtypeStruct((n, d), BF16),
        compiler_params=pltpu.CompilerParams(dimension_semantics=("arbitrary",)),
        name="rmsnorm",
    )(x2, gain.reshape(gain.shape[0], 1, d))


def _mm_body(*refs, pairs, n_a, n_w, n_e, epilogue, w_rows_are_outputs):
    a_refs = refs[:n_a]
    w_refs = refs[n_a:n_a + n_w]
    e_refs = refs[n_a + n_w:n_a + n_w + n_e]
    o_ref = refs[-1]
    dims = (((1,), (1,)), ((), ())) if w_rows_are_outputs else (((1,), (0,)), ((), ()))
    accs = [lax.dot_general(a_refs[ai][...], w_refs[wi][...].astype(BF16), dims, preferred_element_type=F32)
            for ai, wi in pairs]
    o_ref[...] = epilogue(accs, [e[...] for e in e_refs]).astype(o_ref.dtype)


def _mm(name, m, n, bm, bn, a_ops, w_ops, e_ops, pairs, epilogue, out_dtype, w_rows_are_outputs=False):
    arrays = [a for a, _ in a_ops] + [w for w, _ in w_ops] + [e for e, _ in e_ops]
    specs = [s for _, s in a_ops] + [s for _, s in w_ops] + [s for _, s in e_ops]
    blocks, single = [], 0
    for arr, spec in a_ops + w_ops + e_ops:
        nb = _nbytes([s for s in spec.block_shape if s is not None], arr.dtype)
        if spec.pipeline_mode is not None and spec.pipeline_mode.buffer_count == 1:
            single += nb
        else:
            blocks.append(nb)
    blocks.append(_nbytes((bm, bn), out_dtype))
    w_bf16 = sum(_nbytes([s for s in spec.block_shape if s is not None], BF16)
                 for w, spec in w_ops if w.dtype != BF16)
    temps = w_bf16 + len(pairs) * _nbytes((bm, bn), F32)
    body = functools.partial(_mm_body, pairs=tuple(pairs), n_a=len(a_ops), n_w=len(w_ops),
                             n_e=len(e_ops), epilogue=epilogue, w_rows_are_outputs=w_rows_are_outputs)
    return pl.pallas_call(
        body,
        grid=(pl.cdiv(n, bn), m // bm),
        in_specs=specs,
        out_specs=pl.BlockSpec((bm, bn), lambda j, i: (i, j)),
        out_shape=jax.ShapeDtypeStruct((m, n), out_dtype),
        compiler_params=pltpu.CompilerParams(
            dimension_semantics=("arbitrary", "arbitrary"),
            vmem_limit_bytes=_vmem_limit(blocks, temps, single)),
        name=name,
    )(*arrays)


def _a_spec(bm, k):
    return pl.BlockSpec((bm, k), lambda j, i: (i, 0))


def _w_spec(k, bn, l, col_block0=0, single_buffer=False):
    mode = pl.Buffered(1) if single_buffer else None
    return pl.BlockSpec((None, k, bn), lambda j, i: (l, 0, j + col_block0), pipeline_mode=mode)


def _row_spec(bn, l, col_block0=0):
    return pl.BlockSpec((None, 1, bn), lambda j, i: (l, 0, j + col_block0))


def _ep_identity(accs, extras):
    return accs[0]


def _ep_residual(accs, extras):
    return extras[0] + accs[0]


def _ep_swiglu(accs, extras):
    g, u = accs
    return (g * _sigmoid(g)) * u


def _ep_merge(accs, extras):
    g0, g1, g2, p0, p1, p2 = accs
    b0, b1, b2 = extras
    return _sigmoid(g0 + b0) * p0 + _sigmoid(g1 + b1) * p1 + _sigmoid(g2 + b2) * p2


def _fgate_body(u_ref, wf_ref, bf_ref, cum_ref, *, ts, seq):
    wf = wf_ref[...].astype(BF16)
    tri = (lax.broadcasted_iota(jnp.int32, (ts, ts), 0) >= lax.broadcasted_iota(jnp.int32, (ts, ts), 1)).astype(BF16)
    pieces = []
    for k in range(seq // ts):
        f = lax.dot_general(u_ref[k * ts:(k + 1) * ts, :], wf, (((1,), (1,)), ((), ())),
                            preferred_element_type=F32) + bf_ref[...]
        lf = jnp.minimum(f, 0.0) - jnp.log1p(jnp.exp(-jnp.abs(f)))
        hi = lf.astype(BF16)
        r1 = lf - hi.astype(F32)
        mid = r1.astype(BF16)
        pieces.append((hi, mid, (r1 - mid.astype(F32)).astype(BF16)))
    carry = jnp.zeros((1, LANES), F32)
    for k, (hi, mid, lo) in enumerate(pieces):
        cum = (jnp.dot(tri, hi, preferred_element_type=F32)
               + jnp.dot(tri, mid, preferred_element_type=F32)
               + jnp.dot(tri, lo, preferred_element_type=F32)) + carry
        cum_ref[k * ts:(k + 1) * ts, :] = cum
        carry = cum[ts - 1:ts, :]


def _fgate(u2, wf, bf, batch, seq):
    return pl.pallas_call(
        functools.partial(_fgate_body, ts=ROW_TILE, seq=seq),
        grid=(batch,),
        in_specs=[pl.BlockSpec((seq, D_MODEL), lambda b: (b, 0)),
                  pl.BlockSpec((LANES, D_MODEL), lambda b: (0, 0)),
                  pl.BlockSpec((1, LANES), lambda b: (0, 0))],
        out_specs=pl.BlockSpec((seq, LANES), lambda b: (b, 0)),
        out_shape=jax.ShapeDtypeStruct((batch * seq, LANES), F32),
        compiler_params=pltpu.CompilerParams(
            dimension_semantics=("arbitrary",),
            vmem_limit_bytes=_vmem_limit([_nbytes((seq, D_MODEL), BF16), _nbytes((LANES, D_MODEL), F32),
                                          _nbytes((seq, LANES), F32)], _nbytes((ROW_TILE, ROW_TILE), F32))),
        name="fgate_cumsum",
    )(u2, wf, bf)


def _shifted(ext_ref, k, ts):
    return ext_ref[pl.ds(SUBLANES - k, ts), :]


def _lru_body(x_ref, g_ref, cw_ref, cb_ref, wa_ref, ba_ref, wx_ref, bx_ref, lam_ref, y_ref,
              ext_ref, h_ref, *, ts):
    @pl.when(pl.program_id(1) == 0)
    def _():
        ext_ref[0:SUBLANES, :] = jnp.zeros((SUBLANES, LRU_WIDTH), F32)
        h_ref[...] = jnp.zeros_like(h_ref)

    x = x_ref[...]
    ext_ref[pl.ds(SUBLANES, ts), :] = x
    cw = cw_ref[...]
    xc = x * cw[LRU_CONV - 1:LRU_CONV, :] + cb_ref[...]
    for k in range(1, LRU_CONV):
        xc = xc + _shifted(ext_ref, k, ts) * cw[LRU_CONV - 1 - k:LRU_CONV - k, :]
    ext_ref[0:SUBLANES, :] = x[ts - SUBLANES:ts, :]

    xcb = xc.astype(BF16)
    ra, ia = [], []
    for n in range(LRU_BLOCKS):
        blk = xcb[:, n * LRU_BLOCK:(n + 1) * LRU_BLOCK]
        ra.append(jnp.dot(blk, wa_ref[n].astype(BF16), preferred_element_type=F32))
        ia.append(jnp.dot(blk, wx_ref[n].astype(BF16), preferred_element_type=F32))
    r = _sigmoid(jnp.concatenate(ra, axis=1) + ba_ref[...])
    i = _sigmoid(jnp.concatenate(ia, axis=1) + bx_ref[...])
    log_a = (-LRU_C) * r * _softplus(-lam_ref[...])
    a = jnp.exp(log_a)
    th = jnp.tanh(log_a)
    one_m_a2 = -2.0 * th / (1.0 - th)
    mult = jnp.where(one_m_a2 > 0.0, one_m_a2 * lax.rsqrt(one_m_a2), 0.0)
    b = mult * (i * xc)

    rowm = lax.broadcasted_iota(jnp.int32, (SUBLANES, LRU_WIDTH), 0)
    h = h_ref[0:1, :]
    outs = []
    for c in range(ts // SUBLANES):
        sl = slice(c * SUBLANES, (c + 1) * SUBLANES)
        ac, bc = a[sl, :], b[sl, :]
        for d in (1, 2, 4):
            m = rowm >= d
            bc = jnp.where(m, ac * pltpu.roll(bc, d, 0) + bc, bc)
            ac = jnp.where(m, ac * pltpu.roll(ac, d, 0), ac)
        hc = ac * h + bc
        outs.append(hc)
        h = hc[SUBLANES - 1:SUBLANES, :]
    h_ref[...] = jnp.broadcast_to(h, h_ref.shape)
    hs = jnp.concatenate(outs, axis=0)
    y_ref[...] = (hs * jax.nn.gelu(g_ref[...])).astype(y_ref.dtype)


def _lru_branch(z3, conv_w, conv_b, w_a, b_a, w_x, b_x, lam, l):
    batch, seq, _ = z3.shape
    ts = SEQ_TILE
    depth = conv_w.shape[0]
    vec = lambda p: p.reshape(depth, 1, LRU_WIDTH)
    vspec = pl.BlockSpec((None, 1, LRU_WIDTH), lambda b, s: (l, 0, 0))
    wspec = pl.BlockSpec((None, LRU_BLOCKS, LRU_BLOCK, LRU_BLOCK), lambda b, s: (l, 0, 0, 0))
    return pl.pallas_call(
        functools.partial(_lru_body, ts=ts),
        grid=(batch, seq // ts),
        in_specs=[pl.BlockSpec((None, ts, LRU_WIDTH), lambda b, s: (b, s, 0)),
                  pl.BlockSpec((None, ts, LRU_WIDTH), lambda b, s: (b, s, 1)),
                  pl.BlockSpec((None, LRU_CONV, LRU_WIDTH), lambda b, s: (l, 0, 0)),
                  vspec, wspec, vspec, wspec, vspec, vspec],
        out_specs=pl.BlockSpec((None, ts, LRU_WIDTH), lambda b, s: (b, s, 0)),
        out_shape=jax.ShapeDtypeStruct((batch, seq, LRU_WIDTH), BF16),
        scratch_shapes=[pltpu.VMEM((ts + SUBLANES, LRU_WIDTH), F32),
                        pltpu.VMEM((SUBLANES, LRU_WIDTH), F32)],
        compiler_params=pltpu.CompilerParams(dimension_semantics=("arbitrary", "arbitrary")),
        name="rglru_branch",
    )(z3, z3, conv_w, vec(conv_b), w_a, vec(b_a), w_x, vec(b_x), vec(lam))


def _sconv_body(b_ref, c_ref, h_ref, w_ref, y_ref, ext_ref, *, ts):
    @pl.when(pl.program_id(1) == 0)
    def _():
        ext_ref[0:SUBLANES, :] = jnp.zeros((SUBLANES, CONV_WIDTH), F32)

    ch = c_ref[...] * h_ref[...]
    ext_ref[pl.ds(SUBLANES, ts), :] = ch
    w = w_ref[...]
    y = ch * w[CONV_K - 1:CONV_K, :]
    for k in range(1, CONV_K):
        y = y + _shifted(ext_ref, k, ts) * w[CONV_K - 1 - k:CONV_K - k, :]
    ext_ref[0:SUBLANES, :] = ch[ts - SUBLANES:ts, :]
    y_ref[...] = (b_ref[...] * y).astype(y_ref.dtype)


def _sconv_branch(zt3, conv_w, l):
    batch, seq, _ = zt3.shape
    ts = SEQ_TILE
    col = lambda c: pl.BlockSpec((None, ts, CONV_WIDTH), lambda b, s: (b, s, c))
    return pl.pallas_call(
        functools.partial(_sconv_body, ts=ts),
        grid=(batch, seq // ts),
        in_specs=[col(0), col(1), col(2),
                  pl.BlockSpec((None, CONV_K, CONV_WIDTH), lambda b, s: (l, 0, 0))],
        out_specs=pl.BlockSpec((None, ts, CONV_WIDTH), lambda b, s: (b, s, 0)),
        out_shape=jax.ShapeDtypeStruct((batch, seq, CONV_WIDTH), BF16),
        scratch_shapes=[pltpu.VMEM((ts + SUBLANES, CONV_WIDTH), F32)],
        compiler_params=pltpu.CompilerParams(dimension_semantics=("arbitrary", "arbitrary")),
        name="short_conv_branch",
    )(zt3, zt3, zt3, conv_w)


def _head_norm(t, gain):
    return t * lax.rsqrt(jnp.mean(t * t, axis=-1, keepdims=True) + EPS) * gain


def _attn_body(q_ref, k_ref, v_ref, cc_ref, cr_ref, qg_ref, kg_ref, o_ref, *, seq, tq):
    log2e = 1.4426950408889634
    qn = (_head_norm(q_ref[...], qg_ref[...]) * (ATT_HEAD_DIM ** -0.5 * log2e)).astype(BF16)
    kn = _head_norm(k_ref[...], kg_ref[...]).astype(BF16)
    v = v_ref[...].astype(BF16)
    cum_k = cr_ref[...] * log2e
    causal = (lax.broadcasted_iota(jnp.int32, (tq, tq), 0) >= lax.broadcasted_iota(jnp.int32, (tq, tq), 1))
    nt = (((1,), (1,)), ((), ()))
    def scores(qi):
        q0, kv = qi * tq, (qi + 1) * tq
        return lax.dot_general(qn[q0:kv, :], kn[:kv, :], nt, preferred_element_type=F32) - cum_k[:, :kv]

    n_q = seq // tq
    t_next = scores(0)
    for qi in range(n_q):
        q0, kv = qi * tq, (qi + 1) * tq
        cum_q = cc_ref[q0:kv, :] * log2e
        t = t_next
        if qi + 1 < n_q:
            t_next = scores(qi + 1)
        td = jnp.where(causal, t[:, q0:kv], -jnp.inf)
        t = td if qi == 0 else jnp.concatenate([t[:, :q0], td], axis=1)
        mt = jnp.max(t, axis=-1, keepdims=True)
        off = cum_q - (mt + cum_q)
        p = jnp.exp2(t + off)
        denom = jnp.sum(p, axis=-1, keepdims=True)
        o = jnp.dot(p.astype(BF16), v[:kv, :], preferred_element_type=F32)
        o_ref[q0:kv, :] = (o / denom).astype(o_ref.dtype)


def _attention(z3, cum_col, cum_row, q_gain, k_gain, l):
    batch, seq, _ = z3.shape
    depth = q_gain.shape[0]
    hd = ATT_HEAD_DIM
    head = lambda base: pl.BlockSpec((None, seq, hd), lambda b, h: (b, 0, base + h))
    gspec = pl.BlockSpec((None, 1, hd), lambda b, h: (l, 0, 0))
    q_base = 2 * LRU_WIDTH // hd
    return pl.pallas_call(
        functools.partial(_attn_body, seq=seq, tq=Q_TILE),
        grid=(batch, ATT_HEADS),
        in_specs=[head(q_base), head(q_base + ATT_HEADS), head(q_base + 2 * ATT_HEADS),
                  pl.BlockSpec((None, None, seq, 1), lambda b, h: (b, h, 0, 0)),
                  pl.BlockSpec((None, None, 1, seq), lambda b, h: (b, h, 0, 0)),
                  gspec, gspec],
        out_specs=pl.BlockSpec((None, seq, hd), lambda b, h: (b, 0, h)),
        out_shape=jax.ShapeDtypeStruct((batch, seq, ATT_WIDTH), BF16),
        compiler_params=pltpu.CompilerParams(
            dimension_semantics=("arbitrary", "arbitrary"),
            vmem_limit_bytes=_vmem_limit([3 * _nbytes((seq, hd), F32), _nbytes((seq, LANES), F32),
                                          _nbytes((SUBLANES, seq), F32), _nbytes((seq, hd), BF16)],
                                         8 * _nbytes((Q_TILE, seq), F32))),
        name="forgetting_attention",
    )(z3, z3, z3, cum_col, cum_row, q_gain.reshape(depth, 1, hd), k_gain.reshape(depth, 1, hd))


def _router_body(x_ref, g_ref, rw_ref, rb_ref, ri_ref, rwt_ref, cnt_ref, carry_ref, *, bm):
    @pl.when(pl.program_id(0) == 0)
    def _():
        carry_ref[...] = jnp.zeros_like(carry_ref)

    x = x_ref[...]
    h = x * lax.rsqrt(jnp.mean(x * x, axis=-1, keepdims=True) + EPS) * g_ref[...]
    logits = jnp.dot(h, rw_ref[...], preferred_element_type=F32,
                     precision=lax.Precision.HIGHEST) + rb_ref[...]
    lane = lax.broadcasted_iota(jnp.int32, (bm, LANES), 1)
    logits = jnp.where(lane < N_EXPERTS, logits, -jnp.inf)
    m1 = jnp.max(logits, axis=-1, keepdims=True)
    i1 = jnp.min(jnp.where(logits == m1, lane, LANES), axis=-1, keepdims=True)
    rest = jnp.where(lane == i1, -jnp.inf, logits)
    m2 = jnp.max(rest, axis=-1, keepdims=True)
    i2 = jnp.min(jnp.where(rest == m2, lane, LANES), axis=-1, keepdims=True)
    e2 = jnp.exp(m2 - m1)
    w1 = 1.0 / (1.0 + e2)
    w2 = e2 / (1.0 + e2)

    onehot = ((lane == i1) | (lane == i2)).astype(BF16)
    row = lax.broadcasted_iota(jnp.int32, (bm, bm), 0)
    col = lax.broadcasted_iota(jnp.int32, (bm, bm), 1)
    tri = (row > col).astype(BF16)
    before = jnp.dot(tri, onehot, preferred_element_type=F32) + carry_ref[0:1, :]
    rank1 = jnp.sum(jnp.where(lane == i1, before, 0.0), axis=-1, keepdims=True).astype(jnp.int32)
    rank2 = jnp.sum(jnp.where(lane == i2, before, 0.0), axis=-1, keepdims=True).astype(jnp.int32)
    total = carry_ref[0:1, :] + jnp.sum(onehot.astype(F32), axis=0, keepdims=True)
    carry_ref[...] = jnp.broadcast_to(total, carry_ref.shape)
    cnt_ref[...] = jnp.broadcast_to(total, cnt_ref.shape)

    ri_ref[...] = jnp.where(lane == 0, i1, jnp.where(lane == 1, i2,
                            jnp.where(lane == 2, rank1, jnp.where(lane == 3, rank2, 0))))
    rwt_ref[...] = jnp.where(lane == 0, w1, jnp.where(lane == 1, w2, 0.0))


def _router(x2, gain, rw, rb, l):
    n, d = x2.shape
    bm = ROW_TILE
    return pl.pallas_call(
        functools.partial(_router_body, bm=bm),
        grid=(n // bm,),
        in_specs=[pl.BlockSpec((bm, d), lambda i: (i, 0)),
                  pl.BlockSpec((None, 1, d), lambda i: (l, 0, 0)),
                  pl.BlockSpec((d, LANES), lambda i: (0, 0)),
                  pl.BlockSpec((1, LANES), lambda i: (0, 0))],
        out_specs=[pl.BlockSpec((bm, LANES), lambda i: (i, 0)),
                   pl.BlockSpec((bm, LANES), lambda i: (i, 0)),
                   pl.BlockSpec((SUBLANES, LANES), lambda i: (0, 0))],
        out_shape=[jax.ShapeDtypeStruct((n, LANES), jnp.int32),
                   jax.ShapeDtypeStruct((n, LANES), F32),
                   jax.ShapeDtypeStruct((SUBLANES, LANES), F32)],
        scratch_shapes=[pltpu.VMEM((SUBLANES, LANES), F32)],
        compiler_params=pltpu.CompilerParams(dimension_semantics=("arbitrary",)),
        name="moe_router",
    )(x2, gain.reshape(gain.shape[0], 1, d), rw, rb)


def _row_copy(src_hbm, dst_vmem, sem, src_row, dst_row):
    return pltpu.make_async_copy(src_hbm.at[pl.ds(src_row, 1), :], dst_vmem.at[pl.ds(dst_row, 1), :], sem)


def _dispatch_body(tok_ref, ng_ref, x_hbm, g_ref, o_ref, buf_ref, sem, *, tg):
    i = pl.program_id(0)
    slot = i & 1
    n_live = ng_ref[0]

    def issue_tile(step, slot_):
        for r in range(tg):
            _row_copy(x_hbm, buf_ref.at[slot_], sem.at[slot_], tok_ref[step * tg + r], r).start(priority=r % 2)

    @pl.when((i == 0) & (n_live > 0))
    def _():
        issue_tile(0, 0)

    @pl.when(i + 1 < n_live)
    def _():
        issue_tile(i + 1, 1 - slot)

    @pl.when(i < n_live)
    def _():
        for r in range(tg):
            _row_copy(x_hbm, buf_ref.at[slot], sem.at[slot], 0, r).wait()
        x = buf_ref[slot]
        h = x * lax.rsqrt(jnp.mean(x * x, axis=-1, keepdims=True) + EPS) * g_ref[...]
        o_ref[...] = h.astype(o_ref.dtype)

    @pl.when(i >= n_live)
    def _():
        o_ref[...] = jnp.zeros_like(o_ref)


def _dispatch(row_token, n_gather_tiles, x2, gain, l, rows):
    n, d = x2.shape
    tg = GATHER_TILE
    return pl.pallas_call(
        functools.partial(_dispatch_body, tg=tg),
        grid_spec=pltpu.PrefetchScalarGridSpec(
            num_scalar_prefetch=2,
            grid=(rows // tg,),
            in_specs=[pl.BlockSpec(memory_space=pl.ANY),
                      pl.BlockSpec((None, 1, d), lambda i, tok, ng: (l, 0, 0))],
            out_specs=pl.BlockSpec((tg, d), lambda i, tok, ng: (i, 0)),
            scratch_shapes=[pltpu.VMEM((2, tg, d), F32), pltpu.SemaphoreType.DMA((2,))]),
        out_shape=jax.ShapeDtypeStruct((rows, d), BF16),
        compiler_params=pltpu.CompilerParams(dimension_semantics=("arbitrary",)),
        name="moe_dispatch_gather",
    )(row_token, n_gather_tiles, x2, gain.reshape(gain.shape[0], 1, d))


N_MOE_TABLES = 7


def _weight_copies(w_hbms, wbuf, sem, m, expert, j, slot, bn):
    col = pl.multiple_of(j * bn, bn)
    return [pltpu.make_async_copy(w.at[m, expert, :, pl.ds(col, bn)], wbuf.at[slot, k], sem.at[slot, k])
            for k, w in enumerate(w_hbms)]


def _moe_step(tables, w_hbms, wbuf, sem, o_ref, compute, *, tm, bn, m, n_j):
    te_ref, tr_ref, nu_ref, fl_ref, gi_ref, ne_ref, ng_ref = tables
    j, t = pl.program_id(0), pl.program_id(1)
    used = t < nu_ref[0]
    half = tr_ref[t] <= tm // 2
    n_groups = ng_ref[0]
    gidx = j * n_groups + gi_ref[t]
    slot = gidx & 1

    @pl.when(used & (fl_ref[t] == 1))
    def _():
        @pl.when(gidx == 0)
        def _():
            for c in _weight_copies(w_hbms, wbuf, sem, m, te_ref[t], 0, 0, bn):
                c.start()
        for c in _weight_copies(w_hbms, wbuf, sem, m, te_ref[t], j, slot, bn):
            c.wait()
        j_next = j + jnp.where(gi_ref[t] == n_groups - 1, 1, 0)

        @pl.when(j_next < n_j)
        def _():
            for c in _weight_copies(w_hbms, wbuf, sem, m, ne_ref[t], j_next, 1 - slot, bn):
                c.start()

    def run(rows):
        compute(rows, slot)
        if rows < tm:
            o_ref[rows:tm, :] = jnp.zeros((tm - rows, o_ref.shape[1]), o_ref.dtype)

    pl.when(used & jnp.logical_not(half))(lambda: run(tm))
    pl.when(used & half)(lambda: run(tm // 2))

    @pl.when(jnp.logical_not(used))
    def _():
        o_ref[...] = jnp.zeros_like(o_ref)


def _moe_up_body(*refs, **cfg):
    tables = refs[:N_MOE_TABLES]
    a_ref, wg_hbm, wu_hbm, o_ref, wbuf, sem = refs[N_MOE_TABLES:]

    def compute(rows, slot):
        a = a_ref[0:rows, :]
        g = jnp.dot(a, wbuf[slot, 0].astype(BF16), preferred_element_type=F32)
        u = jnp.dot(a, wbuf[slot, 1].astype(BF16), preferred_element_type=F32)
        o_ref[0:rows, :] = ((g * _sigmoid(g)) * u).astype(o_ref.dtype)

    _moe_step(tables, (wg_hbm, wu_hbm), wbuf, sem, o_ref, compute, **cfg)


def _moe_down_body(*refs, **cfg):
    tables = refs[:N_MOE_TABLES]
    a_ref, wd_hbm, o_ref, wbuf, sem = refs[N_MOE_TABLES:]

    def compute(rows, slot):
        o_ref[0:rows, :] = jnp.dot(a_ref[0:rows, :], wbuf[slot, 0].astype(BF16), preferred_element_type=F32)

    _moe_step(tables, (wd_hbm,), wbuf, sem, o_ref, compute, **cfg)


def _tile_clamp(t, nu):
    return jnp.maximum(jnp.minimum(t, nu[0] - 1), 0)


def _moe_grouped(name, body, tables, a, weights, m, bn, out_dtype, n_acc):
    rows, k = a.shape
    n_out = weights[0].shape[-1]
    tm = MOE_TILE
    n_j = n_out // bn
    a_map = lambda j, t, te, tr, nu, fl, gi, ne, ng: (_tile_clamp(t, nu), 0)
    o_map = lambda j, t, te, tr, nu, fl, gi, ne, ng: (t, j)
    blocks = [_nbytes((tm, k), BF16), _nbytes((tm, bn), out_dtype)]
    wbuf_bytes = 2 * len(weights) * _nbytes((k, bn), F32)
    temps = len(weights) * _nbytes((k, bn), BF16) + n_acc * _nbytes((tm, bn), F32)
    return pl.pallas_call(
        functools.partial(body, tm=tm, bn=bn, m=m, n_j=n_j),
        grid_spec=pltpu.PrefetchScalarGridSpec(
            num_scalar_prefetch=N_MOE_TABLES,
            grid=(n_j, rows // tm),
            in_specs=[pl.BlockSpec((tm, k), a_map)] + [pl.BlockSpec(memory_space=pl.ANY)] * len(weights),
            out_specs=pl.BlockSpec((tm, bn), o_map),
            scratch_shapes=[pltpu.VMEM((2, len(weights), k, bn), F32),
                            pltpu.SemaphoreType.DMA((2, len(weights)))]),
        out_shape=jax.ShapeDtypeStruct((rows, n_out), out_dtype),
        compiler_params=pltpu.CompilerParams(dimension_semantics=("arbitrary", "arbitrary"),
                                             vmem_limit_bytes=_vmem_limit(blocks, temps, wbuf_bytes)),
        name=name,
    )(*tables, a, *weights)


def _moe_up(tables, xs, wg, wu, m):
    return _moe_grouped("moe_up_swiglu", _moe_up_body, tables, xs, (wg, wu), m, 1024, BF16, 3)


def _moe_down(tables, act, wd, m):
    return _moe_grouped("moe_down", _moe_down_body, tables, act, (wd,), m, 512, F32, 2)


def _combine_body(pos_ref, y_hbm, x_ref, w_ref, o_ref, buf_ref, sem, *, tc, n_steps):
    i = pl.program_id(0)
    slot = i & 1

    def issue_tile(step, slot_):
        base = step * (tc * TOP_K)
        for r in range(tc):
            for k in range(TOP_K):
                _row_copy(y_hbm, buf_ref.at[slot_, k], sem.at[slot_],
                          pos_ref[base + r * TOP_K + k], r).start(priority=k)

    @pl.when(i == 0)
    def _():
        issue_tile(0, 0)

    @pl.when(i + 1 < n_steps)
    def _():
        issue_tile(i + 1, 1 - slot)

    for r in range(tc):
        for k in range(TOP_K):
            _row_copy(y_hbm, buf_ref.at[slot, k], sem.at[slot], 0, r).wait()
    w = w_ref[...]
    o_ref[...] = x_ref[...] + w[:, 0:1] * buf_ref[slot, 0] + w[:, 1:2] * buf_ref[slot, 1]


def _combine(pos_flat, y, x2, route_w):
    n, d = x2.shape
    tc = GATHER_TILE
    return pl.pallas_call(
        functools.partial(_combine_body, tc=tc, n_steps=n // tc),
        grid_spec=pltpu.PrefetchScalarGridSpec(
            num_scalar_prefetch=1,
            grid=(n // tc,),
            in_specs=[pl.BlockSpec(memory_space=pl.ANY),
                      pl.BlockSpec((tc, d), lambda i, pos: (i, 0)),
                      pl.BlockSpec((tc, LANES), lambda i, pos: (i, 0))],
            out_specs=pl.BlockSpec((tc, d), lambda i, pos: (i, 0)),
            scratch_shapes=[pltpu.VMEM((2, TOP_K, tc, d), F32), pltpu.SemaphoreType.DMA((2,))]),
        out_shape=jax.ShapeDtypeStruct((n, d), F32),
        compiler_params=pltpu.CompilerParams(
            dimension_semantics=("arbitrary",),
            vmem_limit_bytes=_vmem_limit([2 * _nbytes((tc, d), F32), _nbytes((tc, LANES), F32)],
                                         2 * TOP_K * _nbytes((tc, d), F32))),
        name="moe_combine_gather",
    )(pos_flat, y, x2, route_w)


def _routing_tables(route_i, counts, n_tokens, rows):
    tm = MOE_TILE
    expert = route_i[:, 0:TOP_K]
    rank = route_i[:, TOP_K:2 * TOP_K]
    cnt = counts[0, :N_EXPERTS].astype(jnp.int32)
    tiles_e = (cnt + (tm - 1)) // tm
    tile_end = jnp.cumsum(tiles_e)
    row_off = (tile_end - tiles_e) * tm
    pos = row_off[expert] + rank
    token = jnp.broadcast_to(jnp.arange(n_tokens, dtype=jnp.int32)[:, None], pos.shape)
    row_token = jnp.zeros((rows,), jnp.int32).at[pos.reshape(-1)].set(token.reshape(-1), unique_indices=True)
    tiles = jnp.arange(rows // tm, dtype=jnp.int32)
    tile_expert = jnp.minimum(jnp.sum(tiles[:, None] >= tile_end[None, :], axis=1),
                              N_EXPERTS - 1).astype(jnp.int32)
    tile_start = tile_end - tiles_e
    tile_rows = jnp.clip(cnt[tile_expert] - (tiles - tile_start[tile_expert]) * tm, 0, tm)
    n_used = tile_end[N_EXPERTS - 1:N_EXPERTS].astype(jnp.int32)
    experts = jnp.arange(N_EXPERTS, dtype=jnp.int32)
    nonempty = tiles_e > 0
    later = jnp.where(nonempty[None, :] & (experts[None, :] > experts[:, None]), experts[None, :], N_EXPERTS)
    first_group = jnp.min(jnp.where(nonempty, experts, N_EXPERTS))
    next_e = jnp.min(later, axis=1)
    next_e = jnp.where(next_e == N_EXPERTS, first_group, next_e)
    group_of_e = jnp.cumsum(nonempty.astype(jnp.int32)) - 1
    tile_first = (tiles == tile_start[tile_expert]).astype(jnp.int32)
    n_groups = jnp.sum(nonempty.astype(jnp.int32)).reshape(1)
    i32 = lambda v: v.astype(jnp.int32)
    tables = (tile_expert, i32(tile_rows), n_used, tile_first, i32(group_of_e[tile_expert]),
              i32(next_e[tile_expert]), i32(n_groups))
    return pos.reshape(-1).astype(jnp.int32), row_token, tables


def _moe_ffn(x2, gain, l, router_w, router_b, wg, wu, wd, m):
    n, d = x2.shape
    rows = n * TOP_K + N_EXPERTS * MOE_TILE
    rw = jnp.pad(router_w[m], ((0, 0), (0, LANES - N_EXPERTS)))
    rb = jnp.pad(router_b[m], (0, LANES - N_EXPERTS)).reshape(1, LANES)
    route_i, route_w, counts = _router(x2, gain, rw, rb, l)
    pos_flat, row_token, tables = _routing_tables(route_i, counts, n, rows)
    n_used = tables[2]
    xs = _dispatch(row_token, n_used * (MOE_TILE // GATHER_TILE), x2, gain, l, rows)
    act = _moe_up(tables, xs, wg, wu, m)
    y = _moe_down(tables, act, wd, m)
    return _combine(pos_flat, y, x2, route_w)


def _dense_ffn(x2, h, wg, wu, wd, m):
    n, d = x2.shape
    dff = wg.shape[-1]
    bm, bn = 1024, 512
    act = _mm("ffn_up_swiglu", n, dff, bm, bn,
              [(h, _a_spec(bm, d))],
              [(wg, _w_spec(d, bn, m)), (wu, _w_spec(d, bn, m))], [],
              [(0, 0), (0, 1)], _ep_swiglu, BF16)
    bm, bn = 512, 512
    return _mm("ffn_down_residual", n, d, bm, bn,
               [(act, _a_spec(bm, dff))],
               [(wd, _w_spec(dff, bn, m))],
               [(x2, pl.BlockSpec((bm, bn), lambda j, i: (i, j)))],
               [(0, 0)], _ep_residual, F32)


def _mixer(x2, l, batch, seq, norm_mix_g, w_in, lru_conv_w, lru_conv_b, lru_wa, lru_ba, lru_wx, lru_bx,
           lru_lambda, fox_bf, q_norm_g, k_norm_g, sc_conv_w, w_branch_lru, w_branch_att, w_branch_conv,
           w_merge, b_merge, w_out, ffn_gain):
    n, d = x2.shape
    depth = w_in.shape[0]
    u = _rmsnorm(x2, norm_mix_g, l)

    w_t = jnp.swapaxes(w_in, 1, 2)
    w_tail_t = w_t[l, COL_TAIL:, :][None]
    t_spec = lambda layer: pl.BlockSpec((None, bn, d), lambda j, i: (layer, j, 0))
    bm, bn = 1024, 1024
    z = _mm("in_proj_main", n, COL_MAIN, bm, bn, [(u, _a_spec(bm, d))], [(w_t, t_spec(l))], [],
            [(0, 0)], _ep_identity, F32, w_rows_are_outputs=True)
    zt = _mm("in_proj_tail", n, 3 * CONV_WIDTH, bm, bn, [(u, _a_spec(bm, d))],
             [(w_tail_t, t_spec(0))], [], [(0, 0)], _ep_identity, F32, w_rows_are_outputs=True)
    z3 = z.reshape(batch, seq, COL_MAIN)
    zt3 = zt.reshape(batch, seq, 3 * CONV_WIDTH)

    wf = jnp.pad(w_t[l, COL_F:COL_TAIL, :], ((0, LANES - ATT_HEADS), (0, 0)))
    bf = jnp.pad(fox_bf[l], (0, LANES - ATT_HEADS)).reshape(1, LANES)
    cum = _fgate(u, wf, bf, batch, seq)[:, :ATT_HEADS].reshape(batch, seq, ATT_HEADS)
    cum_h = cum.transpose(0, 2, 1)
    cum_col = cum_h.reshape(batch, ATT_HEADS, seq, 1)
    cum_row = cum_h.reshape(batch, ATT_HEADS, 1, seq)

    y_lru = _lru_branch(z3, lru_conv_w, lru_conv_b, lru_wa, lru_ba, lru_wx, lru_bx, lru_lambda, l)
    y_att = _attention(z3, cum_col, cum_row, q_norm_g, k_norm_g, l)
    y_conv = _sconv_branch(zt3, sc_conv_w, l)

    bm, bn = 1024, 256
    nb = d // bn
    ws = _w_spec
    merged = _mm(
        "gated_merge", n, d, bm, bn,
        [(u, _a_spec(bm, d)), (y_lru.reshape(n, LRU_WIDTH), _a_spec(bm, LRU_WIDTH)),
         (y_att.reshape(n, ATT_WIDTH), _a_spec(bm, ATT_WIDTH)),
         (y_conv.reshape(n, CONV_WIDTH), _a_spec(bm, CONV_WIDTH))],
        [(w_merge, ws(d, bn, l, 0)), (w_merge, ws(d, bn, l, nb)), (w_merge, ws(d, bn, l, 2 * nb)),
         (w_branch_lru, ws(LRU_WIDTH, bn, l)), (w_branch_att, ws(ATT_WIDTH, bn, l)),
         (w_branch_conv, ws(CONV_WIDTH, bn, l))],
        [(b_merge.reshape(depth, 1, -1), _row_spec(bn, l, g * nb)) for g in range(3)],
        [(0, 0), (0, 1), (0, 2), (1, 3), (2, 4), (3, 5)], _ep_merge, BF16)

    return _out_proj(merged, w_out, x2, l, ffn_gain)


def _out_proj_body(a_ref, w_ref, x_ref, *rest):
    xn = x_ref[...] + jnp.dot(a_ref[...], w_ref[...].astype(BF16), preferred_element_type=F32)
    if len(rest) == 1:
        rest[0][...] = xn
    else:
        g_ref, o_ref, h_ref = rest
        o_ref[...] = xn
        h = xn * lax.rsqrt(jnp.mean(xn * xn, axis=-1, keepdims=True) + EPS) * g_ref[...]
        h_ref[...] = h.astype(h_ref.dtype)


def _out_proj(merged, w_out, x2, l, ffn_gain):
    n, d = x2.shape
    bm = 256
    row = pl.BlockSpec((bm, d), lambda i: (i, 0))
    in_specs = [row, pl.BlockSpec((None, d, d), lambda i: (l, 0, 0), pipeline_mode=pl.Buffered(1)), row]
    args = [merged, w_out, x2]
    out_specs, out_shape = row, jax.ShapeDtypeStruct((n, d), F32)
    if ffn_gain is not None:
        in_specs.append(pl.BlockSpec((None, 1, d), lambda i: (l, 0, 0)))
        args.append(ffn_gain.reshape(ffn_gain.shape[0], 1, d))
        out_specs = [row, row]
        out_shape = [out_shape, jax.ShapeDtypeStruct((n, d), BF16)]
    blocks = [_nbytes((bm, d), BF16), 2 * _nbytes((bm, d), F32), _nbytes((bm, d), BF16)]
    return pl.pallas_call(
        _out_proj_body,
        grid=(n // bm,),
        in_specs=in_specs,
        out_specs=out_specs,
        out_shape=out_shape,
        compiler_params=pltpu.CompilerParams(
            dimension_semantics=("arbitrary",),
            vmem_limit_bytes=_vmem_limit(blocks, _nbytes((d, d), BF16) + 2 * _nbytes((bm, d), F32),
                                         _nbytes((d, d), F32))),
        name="out_proj_residual",
    )(*args)


def kernel(x, norm_mix_g, w_in, lru_conv_w, lru_conv_b, lru_wa, lru_ba, lru_wx, lru_bx, lru_lambda, fox_bf, q_norm_g, k_norm_g, sc_conv_w, w_branch_lru, w_branch_att, w_branch_conv, w_merge, b_merge, w_out, norm_ffn_g, ffn_wg, ffn_wu, ffn_wd, router_w, router_b, moe_wg, moe_wu, moe_wd):
    batch, seq, d = x.shape
    depth = w_in.shape[0]
    x2 = x.reshape(batch * seq, d)
    for l in range(depth):
        dense = l % 2 == 0
        mixed = _mixer(x2, l, batch, seq, norm_mix_g, w_in, lru_conv_w, lru_conv_b, lru_wa, lru_ba, lru_wx,
                       lru_bx, lru_lambda, fox_bf, q_norm_g, k_norm_g, sc_conv_w, w_branch_lru, w_branch_att,
                       w_branch_conv, w_merge, b_merge, w_out, norm_ffn_g if dense else None)
        if dense:
            x2, h = mixed
            x2 = _dense_ffn(x2, h, ffn_wg, ffn_wu, ffn_wd, l // 2)
        else:
            x2 = _moe_ffn(mixed, norm_ffn_g, l, router_w, router_b, moe_wg, moe_wu, moe_wd, l // 2)
    return x2.reshape(batch, seq, d)
```

```python
import functools

import jax
import jax.numpy as jnp
from jax import lax
from jax.experimental import pallas as pl
from jax.experimental.pallas import tpu as pltpu

F32 = jnp.float32
BF16 = jnp.bfloat16

D_MODEL = 2048
LRU_WIDTH = 1024
LRU_BLOCKS = 8
LRU_BLOCK = 128
LRU_CONV = 4
LRU_C = 8.0
ATT_HEADS = 8
ATT_HEAD_DIM = 128
ATT_WIDTH = 1024
CONV_WIDTH = 1024
CONV_K = 3
N_EXPERTS = 8
TOP_K = 2
EPS = 1e-6
COL_MAIN = 2 * LRU_WIDTH + 3 * ATT_WIDTH
COL_F = COL_MAIN
COL_TAIL = COL_MAIN + ATT_HEADS

LANES = 128
SUBLANES = 8
V7X_VMEM_BUDGET = 56 * 1024 * 1024

ROW_TILE = 512
SEQ_TILE = 256
Q_TILE = 256
MOE_TILE = 512
MOE_TAIL_PARTS = 4
GATHER_TILE = 512


def _vmem_limit(block_bytes, temp_bytes=0, single_bytes=0):
    need = 2 * sum(block_bytes) + single_bytes + temp_bytes + (4 << 20)
    return int(min(max(need, 16 << 20), V7X_VMEM_BUDGET))


def _nbytes(shape, dtype):
    n = 1
    for s in shape:
        n *= s
    return n * jnp.dtype(dtype).itemsize


def _softplus(y):
    return jnp.maximum(y, 0.0) + jnp.log1p(jnp.exp(-jnp.abs(y)))


def _sigmoid(y):
    return 0.5 * jnp.tanh(0.5 * y) + 0.5


def _rmsnorm_body(x_ref, g_ref, o_ref):
    x = x_ref[...]
    y = x * lax.rsqrt(jnp.mean(x * x, axis=-1, keepdims=True) + EPS)
    o_ref[...] = (y * g_ref[...]).astype(o_ref.dtype)


def _rmsnorm(x2, gain, l):
    n, d = x2.shape
    return pl.pallas_call(
        _rmsnorm_body,
        grid=(n // ROW_TILE,),
        in_specs=[pl.BlockSpec((ROW_TILE, d), lambda i: (i, 0)),
                  pl.BlockSpec((None, 1, d), lambda i: (l, 0, 0))],
        out_specs=pl.BlockSpec((ROW_TILE, d), lambda i: (i, 0)),
        out_shape=jax.ShapeDtypeStruct((n, d), BF16),
        compiler_params=pltpu.CompilerParams(dimension_semantics=("arbitrary",)),
        name="rmsnorm",
    )(x2, gain.reshape(gain.shape[0], 1, d))


def _mm_body(*refs, pairs, n_a, n_w, n_e, epilogue, w_rows_are_outputs):
    a_refs = refs[:n_a]
    w_refs = refs[n_a:n_a + n_w]
    e_refs = refs[n_a + n_w:n_a + n_w + n_e]
    o_ref = refs[-1]
    dims = (((1,), (1,)), ((), ())) if w_rows_are_outputs else (((1,), (0,)), ((), ()))
    accs = [lax.dot_general(a_refs[ai][...], w_refs[wi][...].astype(BF16), dims, preferred_element_type=F32)
            for ai, wi in pairs]
    o_ref[...] = epilogue(accs, [e[...] for e in e_refs]).astype(o_ref.dtype)


def _mm(name, m, n, bm, bn, a_ops, w_ops, e_ops, pairs, epilogue, out_dtype, w_rows_are_outputs=False):
    arrays = [a for a, _ in a_ops] + [w for w, _ in w_ops] + [e for e, _ in e_ops]
    specs = [s for _, s in a_ops] + [s for _, s in w_ops] + [s for _, s in e_ops]
    blocks, single = [], 0
    for arr, spec in a_ops + w_ops + e_ops:
        nb = _nbytes([s for s in spec.block_shape if s is not None], arr.dtype)
        if spec.pipeline_mode is not None and spec.pipeline_mode.buffer_count == 1:
            single += nb
        else:
            blocks.append(nb)
    blocks.append(_nbytes((bm, bn), out_dtype))
    w_bf16 = sum(_nbytes([s for s in spec.block_shape if s is not None], BF16)
                 for w, spec in w_ops if w.dtype != BF16)
    temps = w_bf16 + len(pairs) * _nbytes((bm, bn), F32)
    body = functools.partial(_mm_body, pairs=tuple(pairs), n_a=len(a_ops), n_w=len(w_ops),
                             n_e=len(e_ops), epilogue=epilogue, w_rows_are_outputs=w_rows_are_outputs)
    return pl.pallas_call(
        body,
        grid=(pl.cdiv(n, bn), m // bm),
        in_specs=specs,
        out_specs=pl.BlockSpec((bm, bn), lambda j, i: (i, j)),
        out_shape=jax.ShapeDtypeStruct((m, n), out_dtype),
        compiler_params=pltpu.CompilerParams(
            dimension_semantics=("arbitrary", "arbitrary"),
            vmem_limit_bytes=_vmem_limit(blocks, temps, single)),
        name=name,
    )(*arrays)


def _a_spec(bm, k):
    return pl.BlockSpec((bm, k), lambda j, i: (i, 0))


def _w_spec(k, bn, l, col_block0=0, single_buffer=False):
    mode = pl.Buffered(1) if single_buffer else None
    return pl.BlockSpec((None, k, bn), lambda j, i: (l, 0, j + col_block0), pipeline_mode=mode)


def _row_spec(bn, l, col_block0=0):
    return pl.BlockSpec((None, 1, bn), lambda j, i: (l, 0, j + col_block0))


def _ep_identity(accs, extras):
    return accs[0]


def _ep_residual(accs, extras):
    return extras[0] + accs[0]


def _ep_swiglu(accs, extras):
    g, u = accs
    return (g * _sigmoid(g)) * u


def _ep_merge(accs, extras):
    g0, g1, g2, p0, p1, p2 = accs
    b0, b1, b2 = extras
    return _sigmoid(g0 + b0) * p0 + _sigmoid(g1 + b1) * p1 + _sigmoid(g2 + b2) * p2


def _fgate_body(u_ref, wf_ref, bf_ref, cum_ref, *, ts, seq):
    wf = wf_ref[...].astype(BF16)
    tri = (lax.broadcasted_iota(jnp.int32, (ts, ts), 0) >= lax.broadcasted_iota(jnp.int32, (ts, ts), 1)).astype(BF16)
    pieces = []
    for k in range(seq // ts):
        f = lax.dot_general(u_ref[k * ts:(k + 1) * ts, :], wf, (((1,), (1,)), ((), ())),
                            preferred_element_type=F32) + bf_ref[...]
        lf = jnp.minimum(f, 0.0) - jnp.log1p(jnp.exp(-jnp.abs(f)))
        hi = lf.astype(BF16)
        r1 = lf - hi.astype(F32)
        mid = r1.astype(BF16)
        pieces.append((hi, mid, (r1 - mid.astype(F32)).astype(BF16)))
    carry = jnp.zeros((1, LANES), F32)
    for k, (hi, mid, lo) in enumerate(pieces):
        cum = (jnp.dot(tri, hi, preferred_element_type=F32)
               + jnp.dot(tri, mid, preferred_element_type=F32)
               + jnp.dot(tri, lo, preferred_element_type=F32)) + carry
        cum_ref[k * ts:(k + 1) * ts, :] = cum
        carry = cum[ts - 1:ts, :]


def _fgate(u2, wf, bf, batch, seq):
    return pl.pallas_call(
        functools.partial(_fgate_body, ts=ROW_TILE, seq=seq),
        grid=(batch,),
        in_specs=[pl.BlockSpec((seq, D_MODEL), lambda b: (b, 0)),
                  pl.BlockSpec((LANES, D_MODEL), lambda b: (0, 0)),
                  pl.BlockSpec((1, LANES), lambda b: (0, 0))],
        out_specs=pl.BlockSpec((seq, LANES), lambda b: (b, 0)),
        out_shape=jax.ShapeDtypeStruct((batch * seq, LANES), F32),
        compiler_params=pltpu.CompilerParams(
            dimension_semantics=("arbitrary",),
            vmem_limit_bytes=_vmem_limit([_nbytes((seq, D_MODEL), BF16), _nbytes((LANES, D_MODEL), F32),
                                          _nbytes((seq, LANES), F32)], _nbytes((ROW_TILE, ROW_TILE), F32))),
        name="fgate_cumsum",
    )(u2, wf, bf)


def _shifted(ext_ref, k, ts):
    return ext_ref[pl.ds(SUBLANES - k, ts), :]


def _lru_body(x_ref, g_ref, cw_ref, cb_ref, wa_ref, ba_ref, wx_ref, bx_ref, lam_ref, y_ref,
              ext_ref, h_ref, *, ts):
    @pl.when(pl.program_id(1) == 0)
    def _():
        ext_ref[0:SUBLANES, :] = jnp.zeros((SUBLANES, LRU_WIDTH), F32)
        h_ref[...] = jnp.zeros_like(h_ref)

    x = x_ref[...]
    ext_ref[pl.ds(SUBLANES, ts), :] = x
    cw = cw_ref[...]
    xc = x * cw[LRU_CONV - 1:LRU_CONV, :] + cb_ref[...]
    for k in range(1, LRU_CONV):
        xc = xc + _shifted(ext_ref, k, ts) * cw[LRU_CONV - 1 - k:LRU_CONV - k, :]
    ext_ref[0:SUBLANES, :] = x[ts - SUBLANES:ts, :]

    xcb = xc.astype(BF16)
    ra, ia = [], []
    for n in range(LRU_BLOCKS):
        blk = xcb[:, n * LRU_BLOCK:(n + 1) * LRU_BLOCK]
        ra.append(jnp.dot(blk, wa_ref[n].astype(BF16), preferred_element_type=F32))
        ia.append(jnp.dot(blk, wx_ref[n].astype(BF16), preferred_element_type=F32))
    r = _sigmoid(jnp.concatenate(ra, axis=1) + ba_ref[...])
    i = _sigmoid(jnp.concatenate(ia, axis=1) + bx_ref[...])
    log_a = (-LRU_C) * r * _softplus(-lam_ref[...])
    a = jnp.exp(log_a)
    th = jnp.tanh(log_a)
    one_m_a2 = -2.0 * th / (1.0 - th)
    mult = jnp.where(one_m_a2 > 0.0, one_m_a2 * lax.rsqrt(one_m_a2), 0.0)
    b = mult * (i * xc)

    rowm = lax.broadcasted_iota(jnp.int32, (SUBLANES, LRU_WIDTH), 0)
    h = h_ref[0:1, :]
    outs = []
    for c in range(ts // SUBLANES):
        sl = slice(c * SUBLANES, (c + 1) * SUBLANES)
        ac, bc = a[sl, :], b[sl, :]
        for d in (1, 2, 4):
            m = rowm >= d
            bc = jnp.where(m, ac * pltpu.roll(bc, d, 0) + bc, bc)
            ac = jnp.where(m, ac * pltpu.roll(ac, d, 0), ac)
        hc = ac * h + bc
        outs.append(hc)
        h = hc[SUBLANES - 1:SUBLANES, :]
    h_ref[...] = jnp.broadcast_to(h, h_ref.shape)
    hs = jnp.concatenate(outs, axis=0)
    y_ref[...] = (hs * jax.nn.gelu(g_ref[...])).astype(y_ref.dtype)


def _conv_branches_body(x_ref, g_ref, cw_ref, cb_ref, wa_ref, ba_ref, wx_ref, bx_ref, lam_ref,
                        sb_ref, sc_ref, sh_ref, sw_ref, y_ref, ysc_ref, ext_ref, h_ref, ext2_ref, *, ts):
    _sconv_body(sb_ref, sc_ref, sh_ref, sw_ref, ysc_ref, ext2_ref, ts=ts)
    _lru_body(x_ref, g_ref, cw_ref, cb_ref, wa_ref, ba_ref, wx_ref, bx_ref, lam_ref, y_ref, ext_ref, h_ref, ts=ts)


def _conv_branches(z3, zt3, conv_w, conv_b, w_a, b_a, w_x, b_x, lam, sc_conv_w, l):
    batch, seq, _ = z3.shape
    ts = SEQ_TILE
    depth = conv_w.shape[0]
    width = LRU_WIDTH
    assert CONV_WIDTH == width
    vec = lambda p: p.reshape(depth, 1, width)
    vspec = pl.BlockSpec((None, 1, width), lambda b, s: (l, 0, 0))
    wspec = pl.BlockSpec((None, LRU_BLOCKS, LRU_BLOCK, LRU_BLOCK), lambda b, s: (l, 0, 0, 0))
    col = lambda c: pl.BlockSpec((None, ts, width), lambda b, s: (b, s, c))
    out = pl.BlockSpec((None, ts, width), lambda b, s: (b, s, 0))
    return pl.pallas_call(
        functools.partial(_conv_branches_body, ts=ts),
        grid=(batch, seq // ts),
        in_specs=[col(0), col(1),
                  pl.BlockSpec((None, LRU_CONV, width), lambda b, s: (l, 0, 0)),
                  vspec, wspec, vspec, wspec, vspec, vspec,
                  col(0), col(1), col(2),
                  pl.BlockSpec((None, CONV_K, width), lambda b, s: (l, 0, 0))],
        out_specs=[out, out],
        out_shape=[jax.ShapeDtypeStruct((batch, seq, width), BF16)] * 2,
        scratch_shapes=[pltpu.VMEM((ts + SUBLANES, width), F32),
                        pltpu.VMEM((SUBLANES, width), F32),
                        pltpu.VMEM((ts + SUBLANES, width), F32)],
        compiler_params=pltpu.CompilerParams(dimension_semantics=("arbitrary", "arbitrary")),
        name="rglru_and_short_conv_branches",
    )(z3, z3, conv_w, vec(conv_b), w_a, vec(b_a), w_x, vec(b_x), vec(lam), zt3, zt3, zt3, sc_conv_w)


def _sconv_body(b_ref, c_ref, h_ref, w_ref, y_ref, ext_ref, *, ts):
    @pl.when(pl.program_id(1) == 0)
    def _():
        ext_ref[0:SUBLANES, :] = jnp.zeros((SUBLANES, CONV_WIDTH), F32)

    ch = c_ref[...] * h_ref[...]
    ext_ref[pl.ds(SUBLANES, ts), :] = ch
    w = w_ref[...]
    y = ch * w[CONV_K - 1:CONV_K, :]
    for k in range(1, CONV_K):
        y = y + _shifted(ext_ref, k, ts) * w[CONV_K - 1 - k:CONV_K - k, :]
    ext_ref[0:SUBLANES, :] = ch[ts - SUBLANES:ts, :]
    y_ref[...] = (b_ref[...] * y).astype(y_ref.dtype)


def _head_norm(t, gain):
    return t * lax.rsqrt(jnp.mean(t * t, axis=-1, keepdims=True) + EPS) * gain


def _attn_body(q_ref, k_ref, v_ref, cc_ref, cr_ref, qg_ref, kg_ref, o_ref, *, seq, tq):
    log2e = 1.4426950408889634
    qn = (_head_norm(q_ref[...], qg_ref[...]) * (ATT_HEAD_DIM ** -0.5 * log2e)).astype(BF16)
    kn = _head_norm(k_ref[...], kg_ref[...]).astype(BF16)
    v = v_ref[...].astype(BF16)
    cum_k = cr_ref[...] * log2e
    causal = (lax.broadcasted_iota(jnp.int32, (tq, tq), 0) >= lax.broadcasted_iota(jnp.int32, (tq, tq), 1))
    nt = (((1,), (1,)), ((), ()))
    def scores(qi):
        q0, kv = qi * tq, (qi + 1) * tq
        return lax.dot_general(qn[q0:kv, :], kn[:kv, :], nt, preferred_element_type=F32) - cum_k[:, :kv]

    n_q = seq // tq
    ahead = [scores(0), scores(1)]
    for qi in range(n_q):
        q0, kv = qi * tq, (qi + 1) * tq
        cum_q = cc_ref[q0:kv, :] * log2e
        t = ahead.pop(0)
        if qi + 2 < n_q:
            ahead.append(scores(qi + 2))
        td = jnp.where(causal, t[:, q0:kv], -jnp.inf)
        t = td if qi == 0 else jnp.concatenate([t[:, :q0], td], axis=1)
        mt = jnp.max(t, axis=-1, keepdims=True)
        off = cum_q - (mt + cum_q)
        p = jnp.exp2(t + off)
        denom = jnp.sum(p, axis=-1, keepdims=True)
        o = jnp.dot(p.astype(BF16), v[:kv, :], preferred_element_type=F32)
        o_ref[q0:kv, :] = (o / denom).astype(o_ref.dtype)


def _attention(z3, cum_col, cum_row, q_gain, k_gain, l):
    batch, seq, _ = z3.shape
    depth = q_gain.shape[0]
    hd = ATT_HEAD_DIM
    head = lambda base: pl.BlockSpec((None, seq, hd), lambda b, h: (b, 0, base + h))
    gspec = pl.BlockSpec((None, 1, hd), lambda b, h: (l, 0, 0))
    q_base = 2 * LRU_WIDTH // hd
    return pl.pallas_call(
        functools.partial(_attn_body, seq=seq, tq=Q_TILE),
        grid=(batch, ATT_HEADS),
        in_specs=[head(q_base), head(q_base + ATT_HEADS), head(q_base + 2 * ATT_HEADS),
                  pl.BlockSpec((None, None, seq, 1), lambda b, h: (b, h, 0, 0)),
                  pl.BlockSpec((None, None, 1, seq), lambda b, h: (b, h, 0, 0)),
                  gspec, gspec],
        out_specs=pl.BlockSpec((None, seq, hd), lambda b, h: (b, 0, h)),
        out_shape=jax.ShapeDtypeStruct((batch, seq, ATT_WIDTH), BF16),
        compiler_params=pltpu.CompilerParams(
            dimension_semantics=("arbitrary", "arbitrary"),
            vmem_limit_bytes=_vmem_limit([3 * _nbytes((seq, hd), F32), _nbytes((seq, LANES), F32),
                                          _nbytes((SUBLANES, seq), F32), _nbytes((seq, hd), BF16)],
                                         8 * _nbytes((Q_TILE, seq), F32))),
        name="forgetting_attention",
    )(z3, z3, z3, cum_col, cum_row, q_gain.reshape(depth, 1, hd), k_gain.reshape(depth, 1, hd))


def _router_body(x_ref, g_ref, rw_ref, rb_ref, ri_ref, rwt_ref, cnt_ref, carry_ref, *, bm):
    @pl.when(pl.program_id(0) == 0)
    def _():
        carry_ref[...] = jnp.zeros_like(carry_ref)

    x = x_ref[...]
    h = x * lax.rsqrt(jnp.mean(x * x, axis=-1, keepdims=True) + EPS) * g_ref[...]
    logits = jnp.dot(h, rw_ref[...], preferred_element_type=F32,
                     precision=lax.Precision.HIGHEST) + rb_ref[...]
    lane = lax.broadcasted_iota(jnp.int32, (bm, LANES), 1)
    logits = jnp.where(lane < N_EXPERTS, logits, -jnp.inf)
    m1 = jnp.max(logits, axis=-1, keepdims=True)
    i1 = jnp.min(jnp.where(logits == m1, lane, LANES), axis=-1, keepdims=True)
    rest = jnp.where(lane == i1, -jnp.inf, logits)
    m2 = jnp.max(rest, axis=-1, keepdims=True)
    i2 = jnp.min(jnp.where(rest == m2, lane, LANES), axis=-1, keepdims=True)
    e2 = jnp.exp(m2 - m1)
    w1 = 1.0 / (1.0 + e2)
    w2 = e2 / (1.0 + e2)

    onehot = ((lane == i1) | (lane == i2)).astype(BF16)
    row = lax.broadcasted_iota(jnp.int32, (bm, bm), 0)
    col = lax.broadcasted_iota(jnp.int32, (bm, bm), 1)
    tri = (row > col).astype(BF16)
    before = jnp.dot(tri, onehot, preferred_element_type=F32) + carry_ref[0:1, :]
    rank1 = jnp.sum(jnp.where(lane == i1, before, 0.0), axis=-1, keepdims=True).astype(jnp.int32)
    rank2 = jnp.sum(jnp.where(lane == i2, before, 0.0), axis=-1, keepdims=True).astype(jnp.int32)
    total = carry_ref[0:1, :] + jnp.sum(onehot.astype(F32), axis=0, keepdims=True)
    carry_ref[...] = jnp.broadcast_to(total, carry_ref.shape)
    cnt_ref[...] = jnp.broadcast_to(total, cnt_ref.shape)

    ri_ref[...] = jnp.where(lane == 0, i1, jnp.where(lane == 1, i2,
                            jnp.where(lane == 2, rank1, jnp.where(lane == 3, rank2, 0))))
    rwt_ref[...] = jnp.where(lane == 0, w1, jnp.where(lane == 1, w2, 0.0))


def _router(x2, gain, rw, rb, l):
    n, d = x2.shape
    bm = ROW_TILE
    return pl.pallas_call(
        functools.partial(_router_body, bm=bm),
        grid=(n // bm,),
        in_specs=[pl.BlockSpec((bm, d), lambda i: (i, 0)),
                  pl.BlockSpec((None, 1, d), lambda i: (l, 0, 0)),
                  pl.BlockSpec((d, LANES), lambda i: (0, 0)),
                  pl.BlockSpec((1, LANES), lambda i: (0, 0))],
        out_specs=[pl.BlockSpec((bm, LANES), lambda i: (i, 0)),
                   pl.BlockSpec((bm, LANES), lambda i: (i, 0)),
                   pl.BlockSpec((SUBLANES, LANES), lambda i: (0, 0))],
        out_shape=[jax.ShapeDtypeStruct((n, LANES), jnp.int32),
                   jax.ShapeDtypeStruct((n, LANES), F32),
                   jax.ShapeDtypeStruct((SUBLANES, LANES), F32)],
        scratch_shapes=[pltpu.VMEM((SUBLANES, LANES), F32)],
        compiler_params=pltpu.CompilerParams(dimension_semantics=("arbitrary",)),
        name="moe_router",
    )(x2, gain.reshape(gain.shape[0], 1, d), rw, rb)


def _row_copy(src_hbm, dst_vmem, sem, src_row, dst_row):
    return pltpu.make_async_copy(src_hbm.at[pl.ds(src_row, 1), :], dst_vmem.at[pl.ds(dst_row, 1), :], sem)


def _dispatch_body(tok_ref, ng_ref, x_hbm, g_ref, o_ref, buf_ref, sem, *, tg):
    i = pl.program_id(0)
    slot = i & 1
    n_live = ng_ref[0]

    def issue_tile(step, slot_):
        for r in range(tg):
            _row_copy(x_hbm, buf_ref.at[slot_], sem.at[slot_], tok_ref[step * tg + r], r).start(priority=r % 2)

    @pl.when((i == 0) & (n_live > 0))
    def _():
        issue_tile(0, 0)

    @pl.when(i + 1 < n_live)
    def _():
        issue_tile(i + 1, 1 - slot)

    @pl.when(i < n_live)
    def _():
        for r in range(tg):
            _row_copy(x_hbm, buf_ref.at[slot], sem.at[slot], 0, r).wait()
        x = buf_ref[slot]
        h = x * lax.rsqrt(jnp.mean(x * x, axis=-1, keepdims=True) + EPS) * g_ref[...]
        o_ref[...] = h.astype(o_ref.dtype)

    @pl.when(i >= n_live)
    def _():
        o_ref[...] = jnp.zeros_like(o_ref)


def _dispatch(row_token, n_gather_tiles, x2, gain, l, rows):
    n, d = x2.shape
    tg = GATHER_TILE
    return pl.pallas_call(
        functools.partial(_dispatch_body, tg=tg),
        grid_spec=pltpu.PrefetchScalarGridSpec(
            num_scalar_prefetch=2,
            grid=(rows // tg,),
            in_specs=[pl.BlockSpec(memory_space=pl.ANY),
                      pl.BlockSpec((None, 1, d), lambda i, tok, ng: (l, 0, 0))],
            out_specs=pl.BlockSpec((tg, d), lambda i, tok, ng: (i, 0)),
            scratch_shapes=[pltpu.VMEM((2, tg, d), F32), pltpu.SemaphoreType.DMA((2,))]),
        out_shape=jax.ShapeDtypeStruct((rows, d), BF16),
        compiler_params=pltpu.CompilerParams(dimension_semantics=("arbitrary",)),
        name="moe_dispatch_gather",
    )(row_token, n_gather_tiles, x2, gain.reshape(gain.shape[0], 1, d))


N_MOE_TABLES = 7


def _weight_copies(w_hbms, wbuf, sem, m, expert, j, slot, bn):
    col = pl.multiple_of(j * bn, bn)
    return [pltpu.make_async_copy(w.at[m, expert, :, pl.ds(col, bn)], wbuf.at[slot, k], sem.at[slot, k])
            for k, w in enumerate(w_hbms)]


def _moe_step(tables, w_hbms, wbuf, sem, o_ref, compute, *, tm, bn, m, n_j):
    te_ref, tr_ref, nu_ref, fl_ref, gi_ref, ne_ref, ng_ref = tables
    j, t = pl.program_id(0), pl.program_id(1)
    used = t < nu_ref[0]
    n_groups = ng_ref[0]
    gidx = j * n_groups + gi_ref[t]
    slot = gidx & 1

    @pl.when(used & (fl_ref[t] == 1))
    def _():
        @pl.when(gidx == 0)
        def _():
            for c in _weight_copies(w_hbms, wbuf, sem, m, te_ref[t], 0, 0, bn):
                c.start()
        for c in _weight_copies(w_hbms, wbuf, sem, m, te_ref[t], j, slot, bn):
            c.wait()
        j_next = j + jnp.where(gi_ref[t] == n_groups - 1, 1, 0)

        @pl.when(j_next < n_j)
        def _():
            for c in _weight_copies(w_hbms, wbuf, sem, m, ne_ref[t], j_next, 1 - slot, bn):
                c.start()

    def run(rows):
        compute(rows, slot)
        if rows < tm:
            o_ref[rows:tm, :] = jnp.zeros((tm - rows, o_ref.shape[1]), o_ref.dtype)

    quarter = tm // MOE_TAIL_PARTS
    parts = (tr_ref[t] + (quarter - 1)) // quarter
    for q in range(1, MOE_TAIL_PARTS + 1):
        pl.when(used & (parts == q))(functools.partial(run, q * quarter))

    @pl.when(jnp.logical_not(used))
    def _():
        o_ref[...] = jnp.zeros_like(o_ref)


def _moe_up_body(*refs, **cfg):
    tables = refs[:N_MOE_TABLES]
    a_ref, wg_hbm, wu_hbm, o_ref, wbuf, sem = refs[N_MOE_TABLES:]

    def compute(rows, slot):
        a = a_ref[0:rows, :]
        g = jnp.dot(a, wbuf[slot, 0].astype(BF16), preferred_element_type=F32)
        u = jnp.dot(a, wbuf[slot, 1].astype(BF16), preferred_element_type=F32)
        o_ref[0:rows, :] = ((g * _sigmoid(g)) * u).astype(o_ref.dtype)

    _moe_step(tables, (wg_hbm, wu_hbm), wbuf, sem, o_ref, compute, **cfg)


def _moe_down_body(*refs, **cfg):
    tables = refs[:N_MOE_TABLES]
    a_ref, wd_hbm, o_ref, wbuf, sem = refs[N_MOE_TABLES:]

    def compute(rows, slot):
        o_ref[0:rows, :] = jnp.dot(a_ref[0:rows, :], wbuf[slot, 0].astype(BF16), preferred_element_type=F32)

    _moe_step(tables, (wd_hbm,), wbuf, sem, o_ref, compute, **cfg)


def _tile_clamp(t, nu):
    return jnp.maximum(jnp.minimum(t, nu[0] - 1), 0)


def _moe_grouped(name, body, tables, a, weights, m, bn, out_dtype, n_acc):
    rows, k = a.shape
    n_out = weights[0].shape[-1]
    tm = MOE_TILE
    n_j = n_out // bn
    a_map = lambda j, t, te, tr, nu, fl, gi, ne, ng: (_tile_clamp(t, nu), 0)
    o_map = lambda j, t, te, tr, nu, fl, gi, ne, ng: (t, j)
    blocks = [_nbytes((tm, k), BF16), _nbytes((tm, bn), out_dtype)]
    wbuf_bytes = 2 * len(weights) * _nbytes((k, bn), F32)
    temps = len(weights) * _nbytes((k, bn), BF16) + n_acc * _nbytes((tm, bn), F32)
    return pl.pallas_call(
        functools.partial(body, tm=tm, bn=bn, m=m, n_j=n_j),
        grid_spec=pltpu.PrefetchScalarGridSpec(
            num_scalar_prefetch=N_MOE_TABLES,
            grid=(n_j, rows // tm),
            in_specs=[pl.BlockSpec((tm, k), a_map)] + [pl.BlockSpec(memory_space=pl.ANY)] * len(weights),
            out_specs=pl.BlockSpec((tm, bn), o_map),
            scratch_shapes=[pltpu.VMEM((2, len(weights), k, bn), F32),
                            pltpu.SemaphoreType.DMA((2, len(weights)))]),
        out_shape=jax.ShapeDtypeStruct((rows, n_out), out_dtype),
        compiler_params=pltpu.CompilerParams(dimension_semantics=("arbitrary", "arbitrary"),
                                             vmem_limit_bytes=_vmem_limit(blocks, temps, wbuf_bytes)),
        name=name,
    )(*tables, a, *weights)


def _moe_up(tables, xs, wg, wu, m):
    return _moe_grouped("moe_up_swiglu", _moe_up_body, tables, xs, (wg, wu), m, 1024, BF16, 3)


def _moe_down(tables, act, wd, m):
    return _moe_grouped("moe_down", _moe_down_body, tables, act, (wd,), m, 512, F32, 2)


def _combine_body(pos_ref, y_hbm, x_ref, w_ref, o_ref, buf_ref, sem, *, tc, n_steps):
    i = pl.program_id(0)
    slot = i & 1

    def issue_tile(step, slot_):
        base = step * (tc * TOP_K)
        for r in range(tc):
            for k in range(TOP_K):
                _row_copy(y_hbm, buf_ref.at[slot_, k], sem.at[slot_],
                          pos_ref[base + r * TOP_K + k], r).start(priority=k)

    @pl.when(i == 0)
    def _():
        issue_tile(0, 0)

    @pl.when(i + 1 < n_steps)
    def _():
        issue_tile(i + 1, 1 - slot)

    for r in range(tc):
        for k in range(TOP_K):
            _row_copy(y_hbm, buf_ref.at[slot, k], sem.at[slot], 0, r).wait()
    w = w_ref[...]
    o_ref[...] = x_ref[...] + w[:, 0:1] * buf_ref[slot, 0] + w[:, 1:2] * buf_ref[slot, 1]


def _combine(pos_flat, y, x2, route_w):
    n, d = x2.shape
    tc = GATHER_TILE
    return pl.pallas_call(
        functools.partial(_combine_body, tc=tc, n_steps=n // tc),
        grid_spec=pltpu.PrefetchScalarGridSpec(
            num_scalar_prefetch=1,
            grid=(n // tc,),
            in_specs=[pl.BlockSpec(memory_space=pl.ANY),
                      pl.BlockSpec((tc, d), lambda i, pos: (i, 0)),
                      pl.BlockSpec((tc, LANES), lambda i, pos: (i, 0))],
            out_specs=pl.BlockSpec((tc, d), lambda i, pos: (i, 0)),
            scratch_shapes=[pltpu.VMEM((2, TOP_K, tc, d), F32), pltpu.SemaphoreType.DMA((2,))]),
        out_shape=jax.ShapeDtypeStruct((n, d), F32),
        compiler_params=pltpu.CompilerParams(
            dimension_semantics=("arbitrary",),
            vmem_limit_bytes=_vmem_limit([2 * _nbytes((tc, d), F32), _nbytes((tc, LANES), F32)],
                                         2 * TOP_K * _nbytes((tc, d), F32))),
        name="moe_combine_gather",
    )(pos_flat, y, x2, route_w)


def _routing_tables(route_i, counts, n_tokens, rows):
    tm = MOE_TILE
    expert = route_i[:, 0:TOP_K]
    rank = route_i[:, TOP_K:2 * TOP_K]
    cnt = counts[0, :N_EXPERTS].astype(jnp.int32)
    tiles_e = (cnt + (tm - 1)) // tm
    tile_end = jnp.cumsum(tiles_e)
    row_off = (tile_end - tiles_e) * tm
    pos = row_off[expert] + rank
    token = jnp.broadcast_to(jnp.arange(n_tokens, dtype=jnp.int32)[:, None], pos.shape)
    row_token = jnp.zeros((rows,), jnp.int32).at[pos.reshape(-1)].set(token.reshape(-1), unique_indices=True)
    tiles = jnp.arange(rows // tm, dtype=jnp.int32)
    tile_expert = jnp.minimum(jnp.sum(tiles[:, None] >= tile_end[None, :], axis=1),
                              N_EXPERTS - 1).astype(jnp.int32)
    tile_start = tile_end - tiles_e
    tile_rows = jnp.clip(cnt[tile_expert] - (tiles - tile_start[tile_expert]) * tm, 0, tm)
    n_used = tile_end[N_EXPERTS - 1:N_EXPERTS].astype(jnp.int32)
    experts = jnp.arange(N_EXPERTS, dtype=jnp.int32)
    nonempty = tiles_e > 0
    later = jnp.where(nonempty[None, :] & (experts[None, :] > experts[:, None]), experts[None, :], N_EXPERTS)
    first_group = jnp.min(jnp.where(nonempty, experts, N_EXPERTS))
    next_e = jnp.min(later, axis=1)
    next_e = jnp.where(next_e == N_EXPERTS, first_group, next_e)
    group_of_e = jnp.cumsum(nonempty.astype(jnp.int32)) - 1
    tile_first = (tiles == tile_start[tile_expert]).astype(jnp.int32)
    n_groups = jnp.sum(nonempty.astype(jnp.int32)).reshape(1)
    i32 = lambda v: v.astype(jnp.int32)
    tables = (tile_expert, i32(tile_rows), n_used, tile_first, i32(group_of_e[tile_expert]),
              i32(next_e[tile_expert]), i32(n_groups))
    return pos.reshape(-1).astype(jnp.int32), row_token, tables


def _moe_ffn(x2, gain, l, router_w, router_b, wg, wu, wd, m):
    n, d = x2.shape
    rows = n * TOP_K + N_EXPERTS * MOE_TILE
    rw = jnp.pad(router_w[m], ((0, 0), (0, LANES - N_EXPERTS)))
    rb = jnp.pad(router_b[m], (0, LANES - N_EXPERTS)).reshape(1, LANES)
    route_i, route_w, counts = _router(x2, gain, rw, rb, l)
    pos_flat, row_token, tables = _routing_tables(route_i, counts, n, rows)
    n_used = tables[2]
    xs = _dispatch(row_token, n_used * (MOE_TILE // GATHER_TILE), x2, gain, l, rows)
    act = _moe_up(tables, xs, wg, wu, m)
    y = _moe_down(tables, act, wd, m)
    return _combine(pos_flat, y, x2, route_w)


def _dense_ffn(x2, h, wg, wu, wd, m):
    n, d = x2.shape
    dff = wg.shape[-1]
    bm, bn = 1024, 512
    act = _mm("ffn_up_swiglu", n, dff, bm, bn,
              [(h, _a_spec(bm, d))],
              [(wg, _w_spec(d, bn, m)), (wu, _w_spec(d, bn, m))], [],
              [(0, 0), (0, 1)], _ep_swiglu, BF16)
    bm, bn = 512, 512
    return _mm("ffn_down_residual", n, d, bm, bn,
               [(act, _a_spec(bm, dff))],
               [(wd, _w_spec(dff, bn, m))],
               [(x2, pl.BlockSpec((bm, bn), lambda j, i: (i, j)))],
               [(0, 0)], _ep_residual, F32)


def _mixer(x2, l, batch, seq, norm_mix_g, w_in, lru_conv_w, lru_conv_b, lru_wa, lru_ba, lru_wx, lru_bx,
           lru_lambda, fox_bf, q_norm_g, k_norm_g, sc_conv_w, w_branch_lru, w_branch_att, w_branch_conv,
           w_merge, b_merge, w_out, ffn_gain):
    n, d = x2.shape
    depth = w_in.shape[0]
    u = _rmsnorm(x2, norm_mix_g, l)

    w_t = jnp.swapaxes(w_in, 1, 2)
    w_tail_t = w_t[l, COL_TAIL:, :][None]
    t_spec = lambda layer: pl.BlockSpec((None, bn, d), lambda j, i: (layer, j, 0))
    bm, bn = 1024, 1024
    z = _mm("in_proj_main", n, COL_MAIN, bm, bn, [(u, _a_spec(bm, d))], [(w_t, t_spec(l))], [],
            [(0, 0)], _ep_identity, F32, w_rows_are_outputs=True)
    zt = _mm("in_proj_tail", n, 3 * CONV_WIDTH, bm, bn, [(u, _a_spec(bm, d))],
             [(w_tail_t, t_spec(0))], [], [(0, 0)], _ep_identity, F32, w_rows_are_outputs=True)
    z3 = z.reshape(batch, seq, COL_MAIN)
    zt3 = zt.reshape(batch, seq, 3 * CONV_WIDTH)

    wf = jnp.pad(w_t[l, COL_F:COL_TAIL, :], ((0, LANES - ATT_HEADS), (0, 0)))
    bf = jnp.pad(fox_bf[l], (0, LANES - ATT_HEADS)).reshape(1, LANES)
    cum = _fgate(u, wf, bf, batch, seq)[:, :ATT_HEADS].reshape(batch, seq, ATT_HEADS)
    cum_h = cum.transpose(0, 2, 1)
    cum_col = cum_h.reshape(batch, ATT_HEADS, seq, 1)
    cum_row = cum_h.reshape(batch, ATT_HEADS, 1, seq)

    y_lru, y_conv = _conv_branches(z3, zt3, lru_conv_w, lru_conv_b, lru_wa, lru_ba, lru_wx, lru_bx, lru_lambda,
                                   sc_conv_w, l)
    y_att = _attention(z3, cum_col, cum_row, q_norm_g, k_norm_g, l)

    bm, bn = 1024, 256
    nb = d // bn
    ws = _w_spec
    merged = _mm(
        "gated_merge", n, d, bm, bn,
        [(u, _a_spec(bm, d)), (y_lru.reshape(n, LRU_WIDTH), _a_spec(bm, LRU_WIDTH)),
         (y_att.reshape(n, ATT_WIDTH), _a_spec(bm, ATT_WIDTH)),
         (y_conv.reshape(n, CONV_WIDTH), _a_spec(bm, CONV_WIDTH))],
        [(w_merge, ws(d, bn, l, 0)), (w_merge, ws(d, bn, l, nb)), (w_merge, ws(d, bn, l, 2 * nb)),
         (w_branch_lru, ws(LRU_WIDTH, bn, l)), (w_branch_att, ws(ATT_WIDTH, bn, l)),
         (w_branch_conv, ws(CONV_WIDTH, bn, l))],
        [(b_merge.reshape(depth, 1, -1), _row_spec(bn, l, g * nb)) for g in range(3)],
        [(0, 0), (0, 1), (0, 2), (1, 3), (2, 4), (3, 5)], _ep_merge, BF16)

    return _out_proj(merged, w_out, x2, l, ffn_gain)


def _out_proj_body(a_ref, w_ref, x_ref, *rest):
    xn = x_ref[...] + jnp.dot(a_ref[...], w_ref[...].astype(BF16), preferred_element_type=F32)
    if len(rest) == 1:
        rest[0][...] = xn
    else:
        g_ref, o_ref, h_ref = rest
        o_ref[...] = xn
        h = xn * lax.rsqrt(jnp.mean(xn * xn, axis=-1, keepdims=True) + EPS) * g_ref[...]
        h_ref[...] = h.astype(h_ref.dtype)


def _out_proj(merged, w_out, x2, l, ffn_gain):
    n, d = x2.shape
    bm = 256
    row = pl.BlockSpec((bm, d), lambda i: (i, 0))
    in_specs = [row, pl.BlockSpec((None, d, d), lambda i: (l, 0, 0), pipeline_mode=pl.Buffered(1)), row]
    args = [merged, w_out, x2]
    out_specs, out_shape = row, jax.ShapeDtypeStruct((n, d), F32)
    if ffn_gain is not None:
        in_specs.append(pl.BlockSpec((None, 1, d), lambda i: (l, 0, 0)))
        args.append(ffn_gain.reshape(ffn_gain.shape[0], 1, d))
        out_specs = [row, row]
        out_shape = [out_shape, jax.ShapeDtypeStruct((n, d), BF16)]
    blocks = [_nbytes((bm, d), BF16), 2 * _nbytes((bm, d), F32), _nbytes((bm, d), BF16)]
    return pl.pallas_call(
        _out_proj_body,
        grid=(n // bm,),
        in_specs=in_specs,
        out_specs=out_specs,
        out_shape=out_shape,
        compiler_params=pltpu.CompilerParams(
            dimension_semantics=("arbitrary",),
            vmem_limit_bytes=_vmem_limit(blocks, _nbytes((d, d), BF16) + 2 * _nbytes((bm, d), F32),
                                         _nbytes((d, d), F32))),
        name="out_proj_residual",
    )(*args)


def kernel(x, norm_mix_g, w_in, lru_conv_w, lru_conv_b, lru_wa, lru_ba, lru_wx, lru_bx, lru_lambda, fox_bf, q_norm_g, k_norm_g, sc_conv_w, w_branch_lru, w_branch_att, w_branch_conv, w_merge, b_merge, w_out, norm_ffn_g, ffn_wg, ffn_wu, ffn_wd, router_w, router_b, moe_wg, moe_wu, moe_wd):
    batch, seq, d = x.shape
    depth = w_in.shape[0]
    x2 = x.reshape(batch * seq, d)
    for l in range(depth):
        dense = l % 2 == 0
        mixed = _mixer(x2, l, batch, seq, norm_mix_g, w_in, lru_conv_w, lru_conv_b, lru_wa, lru_ba, lru_wx,
                       lru_bx, lru_lambda, fox_bf, q_norm_g, k_norm_g, sc_conv_w, w_branch_lru, w_branch_att,
                       w_branch_conv, w_merge, b_merge, w_out, norm_ffn_g if dense else None)
        if dense:
            x2, h = mixed
            x2 = _dense_ffn(x2, h, ffn_wg, ffn_wu, ffn_wd, l // 2)
        else:
            x2 = _moe_ffn(mixed, norm_ffn_g, l, router_w, router_b, moe_wg, moe_wu, moe_wd, l // 2)
    return x2.reshape(batch, seq, d)
```

```python
import functools

import jax
import jax.numpy as jnp
from jax import lax
from jax.experimental import pallas as pl
from jax.experimental.pallas import tpu as pltpu

F32 = jnp.float32
BF16 = jnp.bfloat16

D_MODEL = 2048
LRU_WIDTH = 1024
LRU_BLOCKS = 8
LRU_BLOCK = 128
LRU_CONV = 4
LRU_C = 8.0
ATT_HEADS = 8
ATT_HEAD_DIM = 128
ATT_WIDTH = 1024
CONV_WIDTH = 1024
CONV_K = 3
N_EXPERTS = 8
TOP_K = 2
EPS = 1e-6
COL_MAIN = 2 * LRU_WIDTH + 3 * ATT_WIDTH
COL_F = COL_MAIN
COL_TAIL = COL_MAIN + ATT_HEADS

LANES = 128
SUBLANES = 8
V7X_VMEM_BUDGET = 56 * 1024 * 1024

ROW_TILE = 512
SEQ_TILE = 256
Q_TILE = 256
MOE_TILE = 512
MOE_TAIL_PARTS = 4
GATHER_TILE = 512


def _vmem_limit(block_bytes, temp_bytes=0, single_bytes=0):
    need = 2 * sum(block_bytes) + single_bytes + temp_bytes + (4 << 20)
    return int(min(max(need, 16 << 20), V7X_VMEM_BUDGET))


def _nbytes(shape, dtype):
    n = 1
    for s in shape:
        n *= s
    return n * jnp.dtype(dtype).itemsize


def _softplus(y):
    return jnp.maximum(y, 0.0) + jnp.log1p(jnp.exp(-jnp.abs(y)))


def _sigmoid(y):
    return 0.5 * jnp.tanh(0.5 * y) + 0.5


def _rmsnorm_body(x_ref, g_ref, o_ref):
    x = x_ref[...]
    y = x * lax.rsqrt(jnp.mean(x * x, axis=-1, keepdims=True) + EPS)
    o_ref[...] = (y * g_ref[...]).astype(o_ref.dtype)


def _rmsnorm(x2, gain, l):
    n, d = x2.shape
    return pl.pallas_call(
        _rmsnorm_body,
        grid=(n // ROW_TILE,),
        in_specs=[pl.BlockSpec((ROW_TILE, d), lambda i: (i, 0)),
                  pl.BlockSpec((None, 1, d), lambda i: (l, 0, 0))],
        out_specs=pl.BlockSpec((ROW_TILE, d), lambda i: (i, 0)),
        out_shape=jax.ShapeDtypeStruct((n, d), BF16),
        compiler_params=pltpu.CompilerParams(dimension_semantics=("arbitrary",)),
        name="rmsnorm",
    )(x2, gain.reshape(gain.shape[0], 1, d))


def _mm_body(*refs, pairs, n_a, n_w, n_e, epilogue, w_rows_are_outputs):
    a_refs = refs[:n_a]
    w_refs = refs[n_a:n_a + n_w]
    e_refs = refs[n_a + n_w:n_a + n_w + n_e]
    o_ref = refs[-1]
    dims = (((1,), (1,)), ((), ())) if w_rows_are_outputs else (((1,), (0,)), ((), ()))
    accs = [lax.dot_general(a_refs[ai][...], w_refs[wi][...].astype(BF16), dims, preferred_element_type=F32)
            for ai, wi in pairs]
    o_ref[...] = epilogue(accs, [e[...] for e in e_refs]).astype(o_ref.dtype)


def _mm(name, m, n, bm, bn, a_ops, w_ops, e_ops, pairs, epilogue, out_dtype, w_rows_are_outputs=False):
    arrays = [a for a, _ in a_ops] + [w for w, _ in w_ops] + [e for e, _ in e_ops]
    specs = [s for _, s in a_ops] + [s for _, s in w_ops] + [s for _, s in e_ops]
    blocks, single = [], 0
    for arr, spec in a_ops + w_ops + e_ops:
        nb = _nbytes([s for s in spec.block_shape if s is not None], arr.dtype)
        if spec.pipeline_mode is not None and spec.pipeline_mode.buffer_count == 1:
            single += nb
        else:
            blocks.append(nb)
    blocks.append(_nbytes((bm, bn), out_dtype))
    w_bf16 = sum(_nbytes([s for s in spec.block_shape if s is not None], BF16)
                 for w, spec in w_ops if w.dtype != BF16)
    temps = w_bf16 + len(pairs) * _nbytes((bm, bn), F32)
    body = functools.partial(_mm_body, pairs=tuple(pairs), n_a=len(a_ops), n_w=len(w_ops),
                             n_e=len(e_ops), epilogue=epilogue, w_rows_are_outputs=w_rows_are_outputs)
    return pl.pallas_call(
        body,
        grid=(pl.cdiv(n, bn), m // bm),
        in_specs=specs,
        out_specs=pl.BlockSpec((bm, bn), lambda j, i: (i, j)),
        out_shape=jax.ShapeDtypeStruct((m, n), out_dtype),
        compiler_params=pltpu.CompilerParams(
            dimension_semantics=("arbitrary", "arbitrary"),
            vmem_limit_bytes=_vmem_limit(blocks, temps, single)),
        name=name,
    )(*arrays)


def _a_spec(bm, k):
    return pl.BlockSpec((bm, k), lambda j, i: (i, 0))


def _w_spec(k, bn, l, col_block0=0, single_buffer=False):
    mode = pl.Buffered(1) if single_buffer else None
    return pl.BlockSpec((None, k, bn), lambda j, i: (l, 0, j + col_block0), pipeline_mode=mode)


def _row_spec(bn, l, col_block0=0):
    return pl.BlockSpec((None, 1, bn), lambda j, i: (l, 0, j + col_block0))


def _ep_identity(accs, extras):
    return accs[0]


def _ep_residual(accs, extras):
    return extras[0] + accs[0]


def _ep_swiglu(accs, extras):
    g, u = accs
    return (g * _sigmoid(g)) * u


def _ep_merge(accs, extras):
    g0, g1, g2, p0, p1, p2 = accs
    b0, b1, b2 = extras
    return _sigmoid(g0 + b0) * p0 + _sigmoid(g1 + b1) * p1 + _sigmoid(g2 + b2) * p2


def _fgate_body(u_ref, wf_ref, bf_ref, cum_ref, *, ts, seq):
    wf = wf_ref[...].astype(BF16)
    tri = (lax.broadcasted_iota(jnp.int32, (ts, ts), 0) >= lax.broadcasted_iota(jnp.int32, (ts, ts), 1)).astype(BF16)
    pieces = []
    for k in range(seq // ts):
        f = lax.dot_general(u_ref[k * ts:(k + 1) * ts, :], wf, (((1,), (1,)), ((), ())),
                            preferred_element_type=F32) + bf_ref[...]
        lf = jnp.minimum(f, 0.0) - jnp.log1p(jnp.exp(-jnp.abs(f)))
        hi = lf.astype(BF16)
        r1 = lf - hi.astype(F32)
        mid = r1.astype(BF16)
        pieces.append((hi, mid, (r1 - mid.astype(F32)).astype(BF16)))
    carry = jnp.zeros((1, LANES), F32)
    for k, (hi, mid, lo) in enumerate(pieces):
        cum = (jnp.dot(tri, hi, preferred_element_type=F32)
               + jnp.dot(tri, mid, preferred_element_type=F32)
               + jnp.dot(tri, lo, preferred_element_type=F32)) + carry
        cum_ref[k * ts:(k + 1) * ts, :] = cum
        carry = cum[ts - 1:ts, :]


def _fgate(u2, wf, bf, batch, seq):
    return pl.pallas_call(
        functools.partial(_fgate_body, ts=ROW_TILE, seq=seq),
        grid=(batch,),
        in_specs=[pl.BlockSpec((seq, D_MODEL), lambda b: (b, 0)),
                  pl.BlockSpec((LANES, D_MODEL), lambda b: (0, 0)),
                  pl.BlockSpec((1, LANES), lambda b: (0, 0))],
        out_specs=pl.BlockSpec((seq, LANES), lambda b: (b, 0)),
        out_shape=jax.ShapeDtypeStruct((batch * seq, LANES), F32),
        compiler_params=pltpu.CompilerParams(
            dimension_semantics=("arbitrary",),
            vmem_limit_bytes=_vmem_limit([_nbytes((seq, D_MODEL), BF16), _nbytes((LANES, D_MODEL), F32),
                                          _nbytes((seq, LANES), F32)], _nbytes((ROW_TILE, ROW_TILE), F32))),
        name="fgate_cumsum",
    )(u2, wf, bf)


def _shifted(ext_ref, k, ts):
    return ext_ref[pl.ds(SUBLANES - k, ts), :]


def _lru_body(x_ref, g_ref, cw_ref, cb_ref, wa_ref, ba_ref, wx_ref, bx_ref, lam_ref, y_ref,
              ext_ref, h_ref, *, ts):
    @pl.when(pl.program_id(1) == 0)
    def _():
        ext_ref[0:SUBLANES, :] = jnp.zeros((SUBLANES, LRU_WIDTH), F32)
        h_ref[...] = jnp.zeros_like(h_ref)

    x = x_ref[...]
    ext_ref[pl.ds(SUBLANES, ts), :] = x
    cw = cw_ref[...]
    xc = x * cw[LRU_CONV - 1:LRU_CONV, :] + cb_ref[...]
    for k in range(1, LRU_CONV):
        xc = xc + _shifted(ext_ref, k, ts) * cw[LRU_CONV - 1 - k:LRU_CONV - k, :]
    ext_ref[0:SUBLANES, :] = x[ts - SUBLANES:ts, :]

    xcb = xc.astype(BF16)
    ra, ia = [], []
    for n in range(LRU_BLOCKS):
        blk = xcb[:, n * LRU_BLOCK:(n + 1) * LRU_BLOCK]
        ra.append(jnp.dot(blk, wa_ref[n].astype(BF16), preferred_element_type=F32))
        ia.append(jnp.dot(blk, wx_ref[n].astype(BF16), preferred_element_type=F32))
    r = _sigmoid(jnp.concatenate(ra, axis=1) + ba_ref[...])
    i = _sigmoid(jnp.concatenate(ia, axis=1) + bx_ref[...])
    log_a = (-LRU_C) * r * _softplus(-lam_ref[...])
    a = jnp.exp(log_a)
    th = jnp.tanh(log_a)
    one_m_a2 = -2.0 * th / (1.0 - th)
    mult = jnp.where(one_m_a2 > 0.0, one_m_a2 * lax.rsqrt(one_m_a2), 0.0)
    b = mult * (i * xc)

    rowm = lax.broadcasted_iota(jnp.int32, (SUBLANES, LRU_WIDTH), 0)
    h = h_ref[0:1, :]
    outs = []
    for c in range(ts // SUBLANES):
        sl = slice(c * SUBLANES, (c + 1) * SUBLANES)
        ac, bc = a[sl, :], b[sl, :]
        for d in (1, 2, 4):
            m = rowm >= d
            bc = jnp.where(m, ac * pltpu.roll(bc, d, 0) + bc, bc)
            ac = jnp.where(m, ac * pltpu.roll(ac, d, 0), ac)
        hc = ac * h + bc
        outs.append(hc)
        h = hc[SUBLANES - 1:SUBLANES, :]
    h_ref[...] = jnp.broadcast_to(h, h_ref.shape)
    hs = jnp.concatenate(outs, axis=0)
    y_ref[...] = (hs * jax.nn.gelu(g_ref[...])).astype(y_ref.dtype)


def _conv_branches_body(x_ref, g_ref, cw_ref, cb_ref, wa_ref, ba_ref, wx_ref, bx_ref, lam_ref,
                        sb_ref, sc_ref, sh_ref, sw_ref, y_ref, ysc_ref, ext_ref, h_ref, ext2_ref, *, ts):
    _sconv_body(sb_ref, sc_ref, sh_ref, sw_ref, ysc_ref, ext2_ref, ts=ts)
    _lru_body(x_ref, g_ref, cw_ref, cb_ref, wa_ref, ba_ref, wx_ref, bx_ref, lam_ref, y_ref, ext_ref, h_ref, ts=ts)


def _conv_branches(z3, zt3, conv_w, conv_b, w_a, b_a, w_x, b_x, lam, sc_conv_w, l):
    batch, seq, _ = z3.shape
    ts = SEQ_TILE
    depth = conv_w.shape[0]
    width = LRU_WIDTH
    assert CONV_WIDTH == width
    vec = lambda p: p.reshape(depth, 1, width)
    vspec = pl.BlockSpec((None, 1, width), lambda b, s: (l, 0, 0))
    wspec = pl.BlockSpec((None, LRU_BLOCKS, LRU_BLOCK, LRU_BLOCK), lambda b, s: (l, 0, 0, 0))
    col = lambda c: pl.BlockSpec((None, ts, width), lambda b, s: (b, s, c))
    out = pl.BlockSpec((None, ts, width), lambda b, s: (b, s, 0))
    return pl.pallas_call(
        functools.partial(_conv_branches_body, ts=ts),
        grid=(batch, seq // ts),
        in_specs=[col(0), col(1),
                  pl.BlockSpec((None, LRU_CONV, width), lambda b, s: (l, 0, 0)),
                  vspec, wspec, vspec, wspec, vspec, vspec,
                  col(0), col(1), col(2),
                  pl.BlockSpec((None, CONV_K, width), lambda b, s: (l, 0, 0))],
        out_specs=[out, out],
        out_shape=[jax.ShapeDtypeStruct((batch, seq, width), BF16)] * 2,
        scratch_shapes=[pltpu.VMEM((ts + SUBLANES, width), F32),
                        pltpu.VMEM((SUBLANES, width), F32),
                        pltpu.VMEM((ts + SUBLANES, width), F32)],
        compiler_params=pltpu.CompilerParams(dimension_semantics=("arbitrary", "arbitrary")),
        name="rglru_and_short_conv_branches",
    )(z3, z3, conv_w, vec(conv_b), w_a, vec(b_a), w_x, vec(b_x), vec(lam), zt3, zt3, zt3, sc_conv_w)


def _sconv_body(b_ref, c_ref, h_ref, w_ref, y_ref, ext_ref, *, ts):
    @pl.when(pl.program_id(1) == 0)
    def _():
        ext_ref[0:SUBLANES, :] = jnp.zeros((SUBLANES, CONV_WIDTH), F32)

    ch = c_ref[...] * h_ref[...]
    ext_ref[pl.ds(SUBLANES, ts), :] = ch
    w = w_ref[...]
    y = ch * w[CONV_K - 1:CONV_K, :]
    for k in range(1, CONV_K):
        y = y + _shifted(ext_ref, k, ts) * w[CONV_K - 1 - k:CONV_K - k, :]
    ext_ref[0:SUBLANES, :] = ch[ts - SUBLANES:ts, :]
    y_ref[...] = (b_ref[...] * y).astype(y_ref.dtype)


def _head_norm(t, gain):
    return t * lax.rsqrt(jnp.mean(t * t, axis=-1, keepdims=True) + EPS) * gain


def _attn_body(q_ref, k_ref, v_ref, cc_ref, cr_ref, qg_ref, kg_ref, o_ref, *, seq, tq):
    log2e = 1.4426950408889634
    qn = (_head_norm(q_ref[...], qg_ref[...]) * (ATT_HEAD_DIM ** -0.5 * log2e)).astype(BF16)
    kn = _head_norm(k_ref[...], kg_ref[...]).astype(BF16)
    v = v_ref[...].astype(BF16)
    cum_k = cr_ref[...] * log2e
    causal = (lax.broadcasted_iota(jnp.int32, (tq, tq), 0) >= lax.broadcasted_iota(jnp.int32, (tq, tq), 1))
    nt = (((1,), (1,)), ((), ()))
    def scores(qi):
        q0, kv = qi * tq, (qi + 1) * tq
        return lax.dot_general(qn[q0:kv, :], kn[:kv, :], nt, preferred_element_type=F32) - cum_k[:, :kv]

    n_q = seq // tq
    ahead = [scores(0), scores(1)]
    for qi in range(n_q):
        q0, kv = qi * tq, (qi + 1) * tq
        cum_q = cc_ref[q0:kv, :] * log2e
        t = ahead.pop(0)
        if qi + 2 < n_q:
            ahead.append(scores(qi + 2))
        td = jnp.where(causal, t[:, q0:kv], -jnp.inf)
        t = td if qi == 0 else jnp.concatenate([t[:, :q0], td], axis=1)
        mt = jnp.max(t, axis=-1, keepdims=True)
        off = cum_q - (mt + cum_q)
        p = jnp.exp2(t + off)
        denom = jnp.sum(p, axis=-1, keepdims=True)
        o = jnp.dot(p.astype(BF16), v[:kv, :], preferred_element_type=F32)
        o_ref[q0:kv, :] = (o / denom).astype(o_ref.dtype)


def _attention(z3, cum_col, cum_row, q_gain, k_gain, l):
    batch, seq, _ = z3.shape
    depth = q_gain.shape[0]
    hd = ATT_HEAD_DIM
    head = lambda base: pl.BlockSpec((None, seq, hd), lambda b, h: (b, 0, base + h))
    gspec = pl.BlockSpec((None, 1, hd), lambda b, h: (l, 0, 0))
    q_base = 2 * LRU_WIDTH // hd
    return pl.pallas_call(
        functools.partial(_attn_body, seq=seq, tq=Q_TILE),
        grid=(batch, ATT_HEADS),
        in_specs=[head(q_base), head(q_base + ATT_HEADS), head(q_base + 2 * ATT_HEADS),
                  pl.BlockSpec((None, None, seq, 1), lambda b, h: (b, h, 0, 0)),
                  pl.BlockSpec((None, None, 1, seq), lambda b, h: (b, h, 0, 0)),
                  gspec, gspec],
        out_specs=pl.BlockSpec((None, seq, hd), lambda b, h: (b, 0, h)),
        out_shape=jax.ShapeDtypeStruct((batch, seq, ATT_WIDTH), BF16),
        compiler_params=pltpu.CompilerParams(
            dimension_semantics=("arbitrary", "arbitrary"),
            vmem_limit_bytes=_vmem_limit([3 * _nbytes((seq, hd), F32), _nbytes((seq, LANES), F32),
                                          _nbytes((SUBLANES, seq), F32), _nbytes((seq, hd), BF16)],
                                         8 * _nbytes((Q_TILE, seq), F32))),
        name="forgetting_attention",
    )(z3, z3, z3, cum_col, cum_row, q_gain.reshape(depth, 1, hd), k_gain.reshape(depth, 1, hd))


def _router_body(x_ref, g_ref, rw_ref, rb_ref, ri_ref, rwt_ref, cnt_ref, carry_ref, *, bm):
    @pl.when(pl.program_id(0) == 0)
    def _():
        carry_ref[...] = jnp.zeros_like(carry_ref)

    x = x_ref[...]
    h = x * lax.rsqrt(jnp.mean(x * x, axis=-1, keepdims=True) + EPS) * g_ref[...]
    rw = rw_ref[...]
    h_hi, w_hi = h.astype(BF16), rw.astype(BF16)
    h_lo = (h - h_hi.astype(F32)).astype(BF16)
    w_lo = (rw - w_hi.astype(F32)).astype(BF16)
    logits = (jnp.dot(h_hi, w_hi, preferred_element_type=F32)
              + (jnp.dot(h_lo, w_hi, preferred_element_type=F32)
                 + jnp.dot(h_hi, w_lo, preferred_element_type=F32))) + rb_ref[...]
    lane = lax.broadcasted_iota(jnp.int32, (bm, LANES), 1)
    logits = jnp.where(lane < N_EXPERTS, logits, -jnp.inf)
    m1 = jnp.max(logits, axis=-1, keepdims=True)
    i1 = jnp.min(jnp.where(logits == m1, lane, LANES), axis=-1, keepdims=True)
    rest = jnp.where(lane == i1, -jnp.inf, logits)
    m2 = jnp.max(rest, axis=-1, keepdims=True)
    i2 = jnp.min(jnp.where(rest == m2, lane, LANES), axis=-1, keepdims=True)
    e2 = jnp.exp(m2 - m1)
    w1 = 1.0 / (1.0 + e2)
    w2 = e2 / (1.0 + e2)

    onehot = ((lane == i1) | (lane == i2)).astype(BF16)
    row = lax.broadcasted_iota(jnp.int32, (bm, bm), 0)
    col = lax.broadcasted_iota(jnp.int32, (bm, bm), 1)
    tri = (row > col).astype(BF16)
    before = jnp.dot(tri, onehot, preferred_element_type=F32) + carry_ref[0:1, :]
    rank1 = jnp.sum(jnp.where(lane == i1, before, 0.0), axis=-1, keepdims=True).astype(jnp.int32)
    rank2 = jnp.sum(jnp.where(lane == i2, before, 0.0), axis=-1, keepdims=True).astype(jnp.int32)
    total = carry_ref[0:1, :] + jnp.sum(onehot.astype(F32), axis=0, keepdims=True)
    carry_ref[...] = jnp.broadcast_to(total, carry_ref.shape)
    cnt_ref[...] = jnp.broadcast_to(total, cnt_ref.shape)

    ri_ref[...] = jnp.where(lane == 0, i1, jnp.where(lane == 1, i2,
                            jnp.where(lane == 2, rank1, jnp.where(lane == 3, rank2, 0))))
    rwt_ref[...] = jnp.where(lane == 0, w1, jnp.where(lane == 1, w2, 0.0))


def _router(x2, gain, rw, rb, l):
    n, d = x2.shape
    bm = ROW_TILE
    return pl.pallas_call(
        functools.partial(_router_body, bm=bm),
        grid=(n // bm,),
        in_specs=[pl.BlockSpec((bm, d), lambda i: (i, 0)),
                  pl.BlockSpec((None, 1, d), lambda i: (l, 0, 0)),
                  pl.BlockSpec((d, LANES), lambda i: (0, 0)),
                  pl.BlockSpec((1, LANES), lambda i: (0, 0))],
        out_specs=[pl.BlockSpec((bm, LANES), lambda i: (i, 0)),
                   pl.BlockSpec((bm, LANES), lambda i: (i, 0)),
                   pl.BlockSpec((SUBLANES, LANES), lambda i: (0, 0))],
        out_shape=[jax.ShapeDtypeStruct((n, LANES), jnp.int32),
                   jax.ShapeDtypeStruct((n, LANES), F32),
                   jax.ShapeDtypeStruct((SUBLANES, LANES), F32)],
        scratch_shapes=[pltpu.VMEM((SUBLANES, LANES), F32)],
        compiler_params=pltpu.CompilerParams(dimension_semantics=("arbitrary",)),
        name="moe_router",
    )(x2, gain.reshape(gain.shape[0], 1, d), rw, rb)


def _row_copy(src_hbm, dst_vmem, sem, src_row, dst_row):
    return pltpu.make_async_copy(src_hbm.at[pl.ds(src_row, 1), :], dst_vmem.at[pl.ds(dst_row, 1), :], sem)


def _dispatch_body(tok_ref, ng_ref, x_hbm, g_ref, o_ref, buf_ref, sem, *, tg):
    i = pl.program_id(0)
    slot = i & 1
    n_live = ng_ref[0]

    def issue_tile(step, slot_):
        for r in range(tg):
            _row_copy(x_hbm, buf_ref.at[slot_], sem.at[slot_], tok_ref[step * tg + r], r).start(priority=r % 2)

    @pl.when((i == 0) & (n_live > 0))
    def _():
        issue_tile(0, 0)

    @pl.when(i + 1 < n_live)
    def _():
        issue_tile(i + 1, 1 - slot)

    @pl.when(i < n_live)
    def _():
        for r in range(tg):
            _row_copy(x_hbm, buf_ref.at[slot], sem.at[slot], 0, r).wait()
        x = buf_ref[slot]
        h = x * lax.rsqrt(jnp.mean(x * x, axis=-1, keepdims=True) + EPS) * g_ref[...]
        o_ref[...] = h.astype(o_ref.dtype)

    @pl.when(i >= n_live)
    def _():
        o_ref[...] = jnp.zeros_like(o_ref)


def _dispatch(row_token, n_gather_tiles, x2, gain, l, rows):
    n, d = x2.shape
    tg = GATHER_TILE
    return pl.pallas_call(
        functools.partial(_dispatch_body, tg=tg),
        grid_spec=pltpu.PrefetchScalarGridSpec(
            num_scalar_prefetch=2,
            grid=(rows // tg,),
            in_specs=[pl.BlockSpec(memory_space=pl.ANY),
                      pl.BlockSpec((None, 1, d), lambda i, tok, ng: (l, 0, 0))],
            out_specs=pl.BlockSpec((tg, d), lambda i, tok, ng: (i, 0)),
            scratch_shapes=[pltpu.VMEM((2, tg, d), F32), pltpu.SemaphoreType.DMA((2,))]),
        out_shape=jax.ShapeDtypeStruct((rows, d), BF16),
        compiler_params=pltpu.CompilerParams(dimension_semantics=("arbitrary",)),
        name="moe_dispatch_gather",
    )(row_token, n_gather_tiles, x2, gain.reshape(gain.shape[0], 1, d))


N_MOE_TABLES = 7


def _weight_copies(w_hbms, wbuf, sem, m, expert, j, slot, bn):
    col = pl.multiple_of(j * bn, bn)
    return [pltpu.make_async_copy(w.at[m, expert, :, pl.ds(col, bn)], wbuf.at[slot, k], sem.at[slot, k])
            for k, w in enumerate(w_hbms)]


def _moe_step(tables, w_hbms, wbuf, sem, o_ref, compute, *, tm, bn, m, n_j):
    te_ref, tr_ref, nu_ref, fl_ref, gi_ref, ne_ref, ng_ref = tables
    j, t = pl.program_id(0), pl.program_id(1)
    used = t < nu_ref[0]
    n_groups = ng_ref[0]
    gidx = j * n_groups + gi_ref[t]
    slot = gidx & 1

    @pl.when(used & (fl_ref[t] == 1))
    def _():
        @pl.when(gidx == 0)
        def _():
            for c in _weight_copies(w_hbms, wbuf, sem, m, te_ref[t], 0, 0, bn):
                c.start()
        for c in _weight_copies(w_hbms, wbuf, sem, m, te_ref[t], j, slot, bn):
            c.wait()
        j_next = j + jnp.where(gi_ref[t] == n_groups - 1, 1, 0)

        @pl.when(j_next < n_j)
        def _():
            for c in _weight_copies(w_hbms, wbuf, sem, m, ne_ref[t], j_next, 1 - slot, bn):
                c.start()

    def run(rows):
        compute(rows, slot)
        if rows < tm:
            o_ref[rows:tm, :] = jnp.zeros((tm - rows, o_ref.shape[1]), o_ref.dtype)

    quarter = tm // MOE_TAIL_PARTS
    parts = (tr_ref[t] + (quarter - 1)) // quarter
    for q in range(1, MOE_TAIL_PARTS + 1):
        pl.when(used & (parts == q))(functools.partial(run, q * quarter))

    @pl.when(jnp.logical_not(used))
    def _():
        o_ref[...] = jnp.zeros_like(o_ref)


def _moe_up_body(*refs, **cfg):
    tables = refs[:N_MOE_TABLES]
    a_ref, wg_hbm, wu_hbm, o_ref, wbuf, sem = refs[N_MOE_TABLES:]

    def compute(rows, slot):
        a = a_ref[0:rows, :]
        g = jnp.dot(a, wbuf[slot, 0].astype(BF16), preferred_element_type=F32)
        u = jnp.dot(a, wbuf[slot, 1].astype(BF16), preferred_element_type=F32)
        o_ref[0:rows, :] = ((g * _sigmoid(g)) * u).astype(o_ref.dtype)

    _moe_step(tables, (wg_hbm, wu_hbm), wbuf, sem, o_ref, compute, **cfg)


def _moe_down_body(*refs, **cfg):
    tables = refs[:N_MOE_TABLES]
    a_ref, wd_hbm, o_ref, wbuf, sem = refs[N_MOE_TABLES:]

    def compute(rows, slot):
        o_ref[0:rows, :] = jnp.dot(a_ref[0:rows, :], wbuf[slot, 0].astype(BF16), preferred_element_type=F32)

    _moe_step(tables, (wd_hbm,), wbuf, sem, o_ref, compute, **cfg)


def _tile_clamp(t, nu):
    return jnp.maximum(jnp.minimum(t, nu[0] - 1), 0)


def _moe_grouped(name, body, tables, a, weights, m, bn, out_dtype, n_acc):
    rows, k = a.shape
    n_out = weights[0].shape[-1]
    tm = MOE_TILE
    n_j = n_out // bn
    a_map = lambda j, t, te, tr, nu, fl, gi, ne, ng: (_tile_clamp(t, nu), 0)
    o_map = lambda j, t, te, tr, nu, fl, gi, ne, ng: (t, j)
    blocks = [_nbytes((tm, k), BF16), _nbytes((tm, bn), out_dtype)]
    wbuf_bytes = 2 * len(weights) * _nbytes((k, bn), F32)
    temps = len(weights) * _nbytes((k, bn), BF16) + n_acc * _nbytes((tm, bn), F32)
    return pl.pallas_call(
        functools.partial(body, tm=tm, bn=bn, m=m, n_j=n_j),
        grid_spec=pltpu.PrefetchScalarGridSpec(
            num_scalar_prefetch=N_MOE_TABLES,
            grid=(n_j, rows // tm),
            in_specs=[pl.BlockSpec((tm, k), a_map)] + [pl.BlockSpec(memory_space=pl.ANY)] * len(weights),
            out_specs=pl.BlockSpec((tm, bn), o_map),
            scratch_shapes=[pltpu.VMEM((2, len(weights), k, bn), F32),
                            pltpu.SemaphoreType.DMA((2, len(weights)))]),
        out_shape=jax.ShapeDtypeStruct((rows, n_out), out_dtype),
        compiler_params=pltpu.CompilerParams(dimension_semantics=("arbitrary", "arbitrary"),
                                             vmem_limit_bytes=_vmem_limit(blocks, temps, wbuf_bytes)),
        name=name,
    )(*tables, a, *weights)


def _moe_up(tables, xs, wg, wu, m):
    return _moe_grouped("moe_up_swiglu", _moe_up_body, tables, xs, (wg, wu), m, 1024, BF16, 3)


def _moe_down(tables, act, wd, m):
    return _moe_grouped("moe_down", _moe_down_body, tables, act, (wd,), m, 512, F32, 2)


def _combine_body(pos_ref, y_hbm, x_ref, w_ref, o_ref, buf_ref, sem, *, tc, n_steps):
    i = pl.program_id(0)
    slot = i & 1

    def issue_tile(step, slot_):
        base = step * (tc * TOP_K)
        for r in range(tc):
            for k in range(TOP_K):
                _row_copy(y_hbm, buf_ref.at[slot_, k], sem.at[slot_],
                          pos_ref[base + r * TOP_K + k], r).start(priority=k)

    @pl.when(i == 0)
    def _():
        issue_tile(0, 0)

    @pl.when(i + 1 < n_steps)
    def _():
        issue_tile(i + 1, 1 - slot)

    for r in range(tc):
        for k in range(TOP_K):
            _row_copy(y_hbm, buf_ref.at[slot, k], sem.at[slot], 0, r).wait()
    w = w_ref[...]
    o_ref[...] = x_ref[...] + w[:, 0:1] * buf_ref[slot, 0] + w[:, 1:2] * buf_ref[slot, 1]


def _combine(pos_flat, y, x2, route_w):
    n, d = x2.shape
    tc = GATHER_TILE
    return pl.pallas_call(
        functools.partial(_combine_body, tc=tc, n_steps=n // tc),
        grid_spec=pltpu.PrefetchScalarGridSpec(
            num_scalar_prefetch=1,
            grid=(n // tc,),
            in_specs=[pl.BlockSpec(memory_space=pl.ANY),
                      pl.BlockSpec((tc, d), lambda i, pos: (i, 0)),
                      pl.BlockSpec((tc, LANES), lambda i, pos: (i, 0))],
            out_specs=pl.BlockSpec((tc, d), lambda i, pos: (i, 0)),
            scratch_shapes=[pltpu.VMEM((2, TOP_K, tc, d), F32), pltpu.SemaphoreType.DMA((2,))]),
        out_shape=jax.ShapeDtypeStruct((n, d), F32),
        compiler_params=pltpu.CompilerParams(
            dimension_semantics=("arbitrary",),
            vmem_limit_bytes=_vmem_limit([2 * _nbytes((tc, d), F32), _nbytes((tc, LANES), F32)],
                                         2 * TOP_K * _nbytes((tc, d), F32))),
        name="moe_combine_gather",
    )(pos_flat, y, x2, route_w)


def _routing_tables(route_i, counts, n_tokens, rows):
    tm = MOE_TILE
    expert = route_i[:, 0:TOP_K]
    rank = route_i[:, TOP_K:2 * TOP_K]
    cnt = counts[0, :N_EXPERTS].astype(jnp.int32)
    tiles_e = (cnt + (tm - 1)) // tm
    tile_end = jnp.cumsum(tiles_e)
    row_off = (tile_end - tiles_e) * tm
    pos = row_off[expert] + rank
    token = jnp.broadcast_to(jnp.arange(n_tokens, dtype=jnp.int32)[:, None], pos.shape)
    row_token = jnp.zeros((rows,), jnp.int32).at[pos.reshape(-1)].set(token.reshape(-1), unique_indices=True)
    tiles = jnp.arange(rows // tm, dtype=jnp.int32)
    tile_expert = jnp.minimum(jnp.sum(tiles[:, None] >= tile_end[None, :], axis=1),
                              N_EXPERTS - 1).astype(jnp.int32)
    tile_start = tile_end - tiles_e
    tile_rows = jnp.clip(cnt[tile_expert] - (tiles - tile_start[tile_expert]) * tm, 0, tm)
    n_used = tile_end[N_EXPERTS - 1:N_EXPERTS].astype(jnp.int32)
    experts = jnp.arange(N_EXPERTS, dtype=jnp.int32)
    nonempty = tiles_e > 0
    later = jnp.where(nonempty[None, :] & (experts[None, :] > experts[:, None]), experts[None, :], N_EXPERTS)
    first_group = jnp.min(jnp.where(nonempty, experts, N_EXPERTS))
    next_e = jnp.min(later, axis=1)
    next_e = jnp.where(next_e == N_EXPERTS, first_group, next_e)
    group_of_e = jnp.cumsum(nonempty.astype(jnp.int32)) - 1
    tile_first = (tiles == tile_start[tile_expert]).astype(jnp.int32)
    n_groups = jnp.sum(nonempty.astype(jnp.int32)).reshape(1)
    i32 = lambda v: v.astype(jnp.int32)
    tables = (tile_expert, i32(tile_rows), n_used, tile_first, i32(group_of_e[tile_expert]),
              i32(next_e[tile_expert]), i32(n_groups))
    return pos.reshape(-1).astype(jnp.int32), row_token, tables


def _moe_ffn(x2, gain, l, router_w, router_b, wg, wu, wd, m):
    n, d = x2.shape
    rows = n * TOP_K + N_EXPERTS * MOE_TILE
    rw = jnp.pad(router_w[m], ((0, 0), (0, LANES - N_EXPERTS)))
    rb = jnp.pad(router_b[m], (0, LANES - N_EXPERTS)).reshape(1, LANES)
    route_i, route_w, counts = _router(x2, gain, rw, rb, l)
    pos_flat, row_token, tables = _routing_tables(route_i, counts, n, rows)
    n_used = tables[2]
    xs = _dispatch(row_token, n_used * (MOE_TILE // GATHER_TILE), x2, gain, l, rows)
    act = _moe_up(tables, xs, wg, wu, m)
    y = _moe_down(tables, act, wd, m)
    return _combine(pos_flat, y, x2, route_w)


def _dense_ffn(x2, h, wg, wu, wd, m):
    n, d = x2.shape
    dff = wg.shape[-1]
    bm, bn = 1024, 512
    act = _mm("ffn_up_swiglu", n, dff, bm, bn,
              [(h, _a_spec(bm, d))],
              [(wg, _w_spec(d, bn, m)), (wu, _w_spec(d, bn, m))], [],
              [(0, 0), (0, 1)], _ep_swiglu, BF16)
    bm, bn = 512, 512
    return _mm("ffn_down_residual", n, d, bm, bn,
               [(act, _a_spec(bm, dff))],
               [(wd, _w_spec(dff, bn, m))],
               [(x2, pl.BlockSpec((bm, bn), lambda j, i: (i, j)))],
               [(0, 0)], _ep_residual, F32)


def _mixer(x2, l, batch, seq, norm_mix_g, w_in, lru_conv_w, lru_conv_b, lru_wa, lru_ba, lru_wx, lru_bx,
           lru_lambda, fox_bf, q_norm_g, k_norm_g, sc_conv_w, w_branch_lru, w_branch_att, w_branch_conv,
           w_merge, b_merge, w_out, ffn_gain):
    n, d = x2.shape
    depth = w_in.shape[0]
    u = _rmsnorm(x2, norm_mix_g, l)

    w_t = jnp.swapaxes(w_in, 1, 2)
    w_tail_t = w_t[l, COL_TAIL:, :][None]
    t_spec = lambda layer: pl.BlockSpec((None, bn, d), lambda j, i: (layer, j, 0))
    bm, bn = 1024, 1024
    z = _mm("in_proj_main", n, COL_MAIN, bm, bn, [(u, _a_spec(bm, d))], [(w_t, t_spec(l))], [],
            [(0, 0)], _ep_identity, F32, w_rows_are_outputs=True)
    zt = _mm("in_proj_tail", n, 3 * CONV_WIDTH, bm, bn, [(u, _a_spec(bm, d))],
             [(w_tail_t, t_spec(0))], [], [(0, 0)], _ep_identity, F32, w_rows_are_outputs=True)
    z3 = z.reshape(batch, seq, COL_MAIN)
    zt3 = zt.reshape(batch, seq, 3 * CONV_WIDTH)

    wf = jnp.pad(w_t[l, COL_F:COL_TAIL, :], ((0, LANES - ATT_HEADS), (0, 0)))
    bf = jnp.pad(fox_bf[l], (0, LANES - ATT_HEADS)).reshape(1, LANES)
    cum = _fgate(u, wf, bf, batch, seq)[:, :ATT_HEADS].reshape(batch, seq, ATT_HEADS)
    cum_h = cum.transpose(0, 2, 1)
    cum_col = cum_h.reshape(batch, ATT_HEADS, seq, 1)
    cum_row = cum_h.reshape(batch, ATT_HEADS, 1, seq)

    y_lru, y_conv = _conv_branches(z3, zt3, lru_conv_w, lru_conv_b, lru_wa, lru_ba, lru_wx, lru_bx, lru_lambda,
                                   sc_conv_w, l)
    y_att = _attention(z3, cum_col, cum_row, q_norm_g, k_norm_g, l)

    bm, bn = 1024, 256
    nb = d // bn
    ws = _w_spec
    merged = _mm(
        "gated_merge", n, d, bm, bn,
        [(u, _a_spec(bm, d)), (y_lru.reshape(n, LRU_WIDTH), _a_spec(bm, LRU_WIDTH)),
         (y_att.reshape(n, ATT_WIDTH), _a_spec(bm, ATT_WIDTH)),
         (y_conv.reshape(n, CONV_WIDTH), _a_spec(bm, CONV_WIDTH))],
        [(w_merge, ws(d, bn, l, 0)), (w_merge, ws(d, bn, l, nb)), (w_merge, ws(d, bn, l, 2 * nb)),
         (w_branch_lru, ws(LRU_WIDTH, bn, l)), (w_branch_att, ws(ATT_WIDTH, bn, l)),
         (w_branch_conv, ws(CONV_WIDTH, bn, l))],
        [(b_merge.reshape(depth, 1, -1), _row_spec(bn, l, g * nb)) for g in range(3)],
        [(0, 0), (0, 1), (0, 2), (1, 3), (2, 4), (3, 5)], _ep_merge, BF16)

    return _out_proj(merged, w_out, x2, l, ffn_gain)


def _out_proj_body(a_ref, w_ref, x_ref, *rest):
    xn = x_ref[...] + jnp.dot(a_ref[...], w_ref[...].astype(BF16), preferred_element_type=F32)
    if len(rest) == 1:
        rest[0][...] = xn
    else:
        g_ref, o_ref, h_ref = rest
        o_ref[...] = xn
        h = xn * lax.rsqrt(jnp.mean(xn * xn, axis=-1, keepdims=True) + EPS) * g_ref[...]
        h_ref[...] = h.astype(h_ref.dtype)


def _out_proj(merged, w_out, x2, l, ffn_gain):
    n, d = x2.shape
    bm = 256
    row = pl.BlockSpec((bm, d), lambda i: (i, 0))
    in_specs = [row, pl.BlockSpec((None, d, d), lambda i: (l, 0, 0), pipeline_mode=pl.Buffered(1)), row]
    args = [merged, w_out, x2]
    out_specs, out_shape = row, jax.ShapeDtypeStruct((n, d), F32)
    if ffn_gain is not None:
        in_specs.append(pl.BlockSpec((None, 1, d), lambda i: (l, 0, 0)))
        args.append(ffn_gain.reshape(ffn_gain.shape[0], 1, d))
        out_specs = [row, row]
        out_shape = [out_shape, jax.ShapeDtypeStruct((n, d), BF16)]
    blocks = [_nbytes((bm, d), BF16), 2 * _nbytes((bm, d), F32), _nbytes((bm, d), BF16)]
    return pl.pallas_call(
        _out_proj_body,
        grid=(n // bm,),
        in_specs=in_specs,
        out_specs=out_specs,
        out_shape=out_shape,
        compiler_params=pltpu.CompilerParams(
            dimension_semantics=("arbitrary",),
            vmem_limit_bytes=_vmem_limit(blocks, _nbytes((d, d), BF16) + 2 * _nbytes((bm, d), F32),
                                         _nbytes((d, d), F32))),
        name="out_proj_residual",
    )(*args)


def kernel(x, norm_mix_g, w_in, lru_conv_w, lru_conv_b, lru_wa, lru_ba, lru_wx, lru_bx, lru_lambda, fox_bf, q_norm_g, k_norm_g, sc_conv_w, w_branch_lru, w_branch_att, w_branch_conv, w_merge, b_merge, w_out, norm_ffn_g, ffn_wg, ffn_wu, ffn_wd, router_w, router_b, moe_wg, moe_wu, moe_wd):
    batch, seq, d = x.shape
    depth = w_in.shape[0]
    x2 = x.reshape(batch * seq, d)
    for l in range(depth):
        dense = l % 2 == 0
        mixed = _mixer(x2, l, batch, seq, norm_mix_g, w_in, lru_conv_w, lru_conv_b, lru_wa, lru_ba, lru_wx,
                       lru_bx, lru_lambda, fox_bf, q_norm_g, k_norm_g, sc_conv_w, w_branch_lru, w_branch_att,
                       w_branch_conv, w_merge, b_merge, w_out, norm_ffn_g if dense else None)
        if dense:
            x2, h = mixed
            x2 = _dense_ffn(x2, h, ffn_wg, ffn_wu, ffn_wd, l // 2)
        else:
            x2 = _moe_ffn(mixed, norm_ffn_g, l, router_w, router_b, moe_wg, moe_wu, moe_wd, l // 2)
    return x2.reshape(batch, seq, d)
```

```python
import functools

import jax
import jax.numpy as jnp
from jax import lax
from jax.experimental import pallas as pl
from jax.experimental.pallas import tpu as pltpu

F32 = jnp.float32
BF16 = jnp.bfloat16

D_MODEL = 2048
LRU_WIDTH = 1024
LRU_BLOCKS = 8
LRU_BLOCK = 128
LRU_CONV = 4
LRU_C = 8.0
ATT_HEADS = 8
ATT_HEAD_DIM = 128
ATT_WIDTH = 1024
CONV_WIDTH = 1024
CONV_K = 3
N_EXPERTS = 8
TOP_K = 2
EPS = 1e-6
COL_MAIN = 2 * LRU_WIDTH + 3 * ATT_WIDTH
COL_F = COL_MAIN
COL_TAIL = COL_MAIN + ATT_HEADS

LANES = 128
SUBLANES = 8
V7X_VMEM_BUDGET = 56 * 1024 * 1024

ROW_TILE = 512
SEQ_TILE = 256
Q_TILE = 256
MOE_TILE = 512
MOE_TAIL_PARTS = 4
MOE_ACT_SLOTS = 3
GATHER_TILE = 512


def _vmem_limit(block_bytes, temp_bytes=0, single_bytes=0):
    need = 2 * sum(block_bytes) + single_bytes + temp_bytes + (4 << 20)
    return int(min(max(need, 16 << 20), V7X_VMEM_BUDGET))


def _nbytes(shape, dtype):
    n = 1
    for s in shape:
        n *= s
    return n * jnp.dtype(dtype).itemsize


def _softplus(y):
    return jnp.maximum(y, 0.0) + jnp.log1p(jnp.exp(-jnp.abs(y)))


def _sigmoid(y):
    return 0.5 * jnp.tanh(0.5 * y) + 0.5


def _rmsnorm_body(x_ref, g_ref, o_ref):
    x = x_ref[...]
    y = x * lax.rsqrt(jnp.mean(x * x, axis=-1, keepdims=True) + EPS)
    o_ref[...] = (y * g_ref[...]).astype(o_ref.dtype)


def _rmsnorm(x2, gain, l):
    n, d = x2.shape
    return pl.pallas_call(
        _rmsnorm_body,
        grid=(n // ROW_TILE,),
        in_specs=[pl.BlockSpec((ROW_TILE, d), lambda i: (i, 0)),
                  pl.BlockSpec((None, 1, d), lambda i: (l, 0, 0))],
        out_specs=pl.BlockSpec((ROW_TILE, d), lambda i: (i, 0)),
        out_shape=jax.ShapeDtypeStruct((n, d), BF16),
        compiler_params=pltpu.CompilerParams(dimension_semantics=("arbitrary",)),
        name="rmsnorm",
    )(x2, gain.reshape(gain.shape[0], 1, d))


def _mm_body(*refs, pairs, n_a, n_w, n_e, epilogue, w_rows_are_outputs):
    a_refs = refs[:n_a]
    w_refs = refs[n_a:n_a + n_w]
    e_refs = refs[n_a + n_w:n_a + n_w + n_e]
    o_ref = refs[-1]
    dims = (((1,), (1,)), ((), ())) if w_rows_are_outputs else (((1,), (0,)), ((), ()))
    accs = [lax.dot_general(a_refs[ai][...], w_refs[wi][...].astype(BF16), dims, preferred_element_type=F32)
            for ai, wi in pairs]
    o_ref[...] = epilogue(accs, [e[...] for e in e_refs]).astype(o_ref.dtype)


def _mm(name, m, n, bm, bn, a_ops, w_ops, e_ops, pairs, epilogue, out_dtype, w_rows_are_outputs=False):
    arrays = [a for a, _ in a_ops] + [w for w, _ in w_ops] + [e for e, _ in e_ops]
    specs = [s for _, s in a_ops] + [s for _, s in w_ops] + [s for _, s in e_ops]
    blocks, single = [], 0
    for arr, spec in a_ops + w_ops + e_ops:
        nb = _nbytes([s for s in spec.block_shape if s is not None], arr.dtype)
        if spec.pipeline_mode is not None and spec.pipeline_mode.buffer_count == 1:
            single += nb
        else:
            blocks.append(nb)
    blocks.append(_nbytes((bm, bn), out_dtype))
    w_bf16 = sum(_nbytes([s for s in spec.block_shape if s is not None], BF16)
                 for w, spec in w_ops if w.dtype != BF16)
    temps = w_bf16 + len(pairs) * _nbytes((bm, bn), F32)
    body = functools.partial(_mm_body, pairs=tuple(pairs), n_a=len(a_ops), n_w=len(w_ops),
                             n_e=len(e_ops), epilogue=epilogue, w_rows_are_outputs=w_rows_are_outputs)
    return pl.pallas_call(
        body,
        grid=(pl.cdiv(n, bn), m // bm),
        in_specs=specs,
        out_specs=pl.BlockSpec((bm, bn), lambda j, i: (i, j)),
        out_shape=jax.ShapeDtypeStruct((m, n), out_dtype),
        compiler_params=pltpu.CompilerParams(
            dimension_semantics=("arbitrary", "arbitrary"),
            vmem_limit_bytes=_vmem_limit(blocks, temps, single)),
        name=name,
    )(*arrays)


def _a_spec(bm, k):
    return pl.BlockSpec((bm, k), lambda j, i: (i, 0))


def _w_spec(k, bn, l, col_block0=0, single_buffer=False):
    mode = pl.Buffered(1) if single_buffer else None
    return pl.BlockSpec((None, k, bn), lambda j, i: (l, 0, j + col_block0), pipeline_mode=mode)


def _row_spec(bn, l, col_block0=0):
    return pl.BlockSpec((None, 1, bn), lambda j, i: (l, 0, j + col_block0))


def _ep_identity(accs, extras):
    return accs[0]


def _ep_residual(accs, extras):
    return extras[0] + accs[0]


def _ep_swiglu(accs, extras):
    g, u = accs
    return (g * _sigmoid(g)) * u


def _ep_merge(accs, extras):
    g0, g1, g2, p0, p1, p2 = accs
    b0, b1, b2 = extras
    return _sigmoid(g0 + b0) * p0 + _sigmoid(g1 + b1) * p1 + _sigmoid(g2 + b2) * p2


def _fgate_body(u_ref, wf_ref, bf_ref, cum_ref, *, ts, seq):
    wf = wf_ref[...].astype(BF16)
    tri = (lax.broadcasted_iota(jnp.int32, (ts, ts), 0) >= lax.broadcasted_iota(jnp.int32, (ts, ts), 1)).astype(BF16)
    pieces = []
    for k in range(seq // ts):
        f = lax.dot_general(u_ref[k * ts:(k + 1) * ts, :], wf, (((1,), (1,)), ((), ())),
                            preferred_element_type=F32) + bf_ref[...]
        lf = jnp.minimum(f, 0.0) - jnp.log1p(jnp.exp(-jnp.abs(f)))
        hi = lf.astype(BF16)
        r1 = lf - hi.astype(F32)
        mid = r1.astype(BF16)
        pieces.append((hi, mid, (r1 - mid.astype(F32)).astype(BF16)))
    carry = jnp.zeros((1, LANES), F32)
    for k, (hi, mid, lo) in enumerate(pieces):
        cum = (jnp.dot(tri, hi, preferred_element_type=F32)
               + jnp.dot(tri, mid, preferred_element_type=F32)
               + jnp.dot(tri, lo, preferred_element_type=F32)) + carry
        cum_ref[k * ts:(k + 1) * ts, :] = cum
        carry = cum[ts - 1:ts, :]


def _fgate(u2, wf, bf, batch, seq):
    return pl.pallas_call(
        functools.partial(_fgate_body, ts=ROW_TILE, seq=seq),
        grid=(batch,),
        in_specs=[pl.BlockSpec((seq, D_MODEL), lambda b: (b, 0)),
                  pl.BlockSpec((LANES, D_MODEL), lambda b: (0, 0)),
                  pl.BlockSpec((1, LANES), lambda b: (0, 0))],
        out_specs=pl.BlockSpec((seq, LANES), lambda b: (b, 0)),
        out_shape=jax.ShapeDtypeStruct((batch * seq, LANES), F32),
        compiler_params=pltpu.CompilerParams(
            dimension_semantics=("arbitrary",),
            vmem_limit_bytes=_vmem_limit([_nbytes((seq, D_MODEL), BF16), _nbytes((LANES, D_MODEL), F32),
                                          _nbytes((seq, LANES), F32)], _nbytes((ROW_TILE, ROW_TILE), F32))),
        name="fgate_cumsum",
    )(u2, wf, bf)


def _shifted(ext_ref, k, ts):
    return ext_ref[pl.ds(SUBLANES - k, ts), :]


def _lru_body(x_ref, g_ref, cw_ref, cb_ref, wa_ref, ba_ref, wx_ref, bx_ref, lam_ref, y_ref,
              ext_ref, h_ref, *, ts):
    @pl.when(pl.program_id(1) == 0)
    def _():
        ext_ref[0:SUBLANES, :] = jnp.zeros((SUBLANES, LRU_WIDTH), F32)
        h_ref[...] = jnp.zeros_like(h_ref)

    x = x_ref[...]
    ext_ref[pl.ds(SUBLANES, ts), :] = x
    cw = cw_ref[...]
    xc = x * cw[LRU_CONV - 1:LRU_CONV, :] + cb_ref[...]
    for k in range(1, LRU_CONV):
        xc = xc + _shifted(ext_ref, k, ts) * cw[LRU_CONV - 1 - k:LRU_CONV - k, :]
    ext_ref[0:SUBLANES, :] = x[ts - SUBLANES:ts, :]

    xcb = xc.astype(BF16)
    ra, ia = [], []
    for n in range(LRU_BLOCKS):
        blk = xcb[:, n * LRU_BLOCK:(n + 1) * LRU_BLOCK]
        ra.append(jnp.dot(blk, wa_ref[n].astype(BF16), preferred_element_type=F32))
        ia.append(jnp.dot(blk, wx_ref[n].astype(BF16), preferred_element_type=F32))
    r = _sigmoid(jnp.concatenate(ra, axis=1) + ba_ref[...])
    i = _sigmoid(jnp.concatenate(ia, axis=1) + bx_ref[...])
    log_a = (-LRU_C) * r * _softplus(-lam_ref[...])
    a = jnp.exp(log_a)
    th = jnp.tanh(log_a)
    one_m_a2 = -2.0 * th / (1.0 - th)
    mult = jnp.where(one_m_a2 > 0.0, one_m_a2 * lax.rsqrt(one_m_a2), 0.0)
    b = mult * (i * xc)

    rowm = lax.broadcasted_iota(jnp.int32, (SUBLANES, LRU_WIDTH), 0)
    h = h_ref[0:1, :]
    outs = []
    for c in range(ts // SUBLANES):
        sl = slice(c * SUBLANES, (c + 1) * SUBLANES)
        ac, bc = a[sl, :], b[sl, :]
        for d in (1, 2, 4):
            m = rowm >= d
            bc = jnp.where(m, ac * pltpu.roll(bc, d, 0) + bc, bc)
            ac = jnp.where(m, ac * pltpu.roll(ac, d, 0), ac)
        hc = ac * h + bc
        outs.append(hc)
        h = hc[SUBLANES - 1:SUBLANES, :]
    h_ref[...] = jnp.broadcast_to(h, h_ref.shape)
    hs = jnp.concatenate(outs, axis=0)
    y_ref[...] = (hs * jax.nn.gelu(g_ref[...])).astype(y_ref.dtype)


def _conv_branches_body(x_ref, g_ref, cw_ref, cb_ref, wa_ref, ba_ref, wx_ref, bx_ref, lam_ref,
                        sb_ref, sc_ref, sh_ref, sw_ref, y_ref, ysc_ref, ext_ref, h_ref, ext2_ref, *, ts):
    _sconv_body(sb_ref, sc_ref, sh_ref, sw_ref, ysc_ref, ext2_ref, ts=ts)
    _lru_body(x_ref, g_ref, cw_ref, cb_ref, wa_ref, ba_ref, wx_ref, bx_ref, lam_ref, y_ref, ext_ref, h_ref, ts=ts)


def _conv_branches(z3, zt3, conv_w, conv_b, w_a, b_a, w_x, b_x, lam, sc_conv_w, l):
    batch, seq, _ = z3.shape
    ts = SEQ_TILE
    depth = conv_w.shape[0]
    width = LRU_WIDTH
    assert CONV_WIDTH == width
    vec = lambda p: p.reshape(depth, 1, width)
    vspec = pl.BlockSpec((None, 1, width), lambda b, s: (l, 0, 0))
    wspec = pl.BlockSpec((None, LRU_BLOCKS, LRU_BLOCK, LRU_BLOCK), lambda b, s: (l, 0, 0, 0))
    col = lambda c: pl.BlockSpec((None, ts, width), lambda b, s: (b, s, c))
    out = pl.BlockSpec((None, ts, width), lambda b, s: (b, s, 0))
    return pl.pallas_call(
        functools.partial(_conv_branches_body, ts=ts),
        grid=(batch, seq // ts),
        in_specs=[col(0), col(1),
                  pl.BlockSpec((None, LRU_CONV, width), lambda b, s: (l, 0, 0)),
                  vspec, wspec, vspec, wspec, vspec, vspec,
                  col(0), col(1), col(2),
                  pl.BlockSpec((None, CONV_K, width), lambda b, s: (l, 0, 0))],
        out_specs=[out, out],
        out_shape=[jax.ShapeDtypeStruct((batch, seq, width), BF16)] * 2,
        scratch_shapes=[pltpu.VMEM((ts + SUBLANES, width), F32),
                        pltpu.VMEM((SUBLANES, width), F32),
                        pltpu.VMEM((ts + SUBLANES, width), F32)],
        compiler_params=pltpu.CompilerParams(dimension_semantics=("arbitrary", "arbitrary")),
        name="rglru_and_short_conv_branches",
    )(z3, z3, conv_w, vec(conv_b), w_a, vec(b_a), w_x, vec(b_x), vec(lam), zt3, zt3, zt3, sc_conv_w)


def _sconv_body(b_ref, c_ref, h_ref, w_ref, y_ref, ext_ref, *, ts):
    @pl.when(pl.program_id(1) == 0)
    def _():
        ext_ref[0:SUBLANES, :] = jnp.zeros((SUBLANES, CONV_WIDTH), F32)

    ch = c_ref[...] * h_ref[...]
    ext_ref[pl.ds(SUBLANES, ts), :] = ch
    w = w_ref[...]
    y = ch * w[CONV_K - 1:CONV_K, :]
    for k in range(1, CONV_K):
        y = y + _shifted(ext_ref, k, ts) * w[CONV_K - 1 - k:CONV_K - k, :]
    ext_ref[0:SUBLANES, :] = ch[ts - SUBLANES:ts, :]
    y_ref[...] = (b_ref[...] * y).astype(y_ref.dtype)


def _head_norm(t, gain):
    return t * lax.rsqrt(jnp.mean(t * t, axis=-1, keepdims=True) + EPS) * gain


def _attn_body(q_ref, k_ref, v_ref, cc_ref, cr_ref, qg_ref, kg_ref, o_ref, *, seq, tq):
    log2e = 1.4426950408889634
    qn = (_head_norm(q_ref[...], qg_ref[...]) * (ATT_HEAD_DIM ** -0.5 * log2e)).astype(BF16)
    kn = _head_norm(k_ref[...], kg_ref[...]).astype(BF16)
    v = v_ref[...].astype(BF16)
    cum_k = cr_ref[...] * log2e
    causal = (lax.broadcasted_iota(jnp.int32, (tq, tq), 0) >= lax.broadcasted_iota(jnp.int32, (tq, tq), 1))
    nt = (((1,), (1,)), ((), ()))
    def scores(qi):
        q0, kv = qi * tq, (qi + 1) * tq
        return lax.dot_general(qn[q0:kv, :], kn[:kv, :], nt, preferred_element_type=F32) - cum_k[:, :kv]

    n_q = seq // tq
    ahead = [scores(0), scores(1)]
    for qi in range(n_q):
        q0, kv = qi * tq, (qi + 1) * tq
        cum_q = cc_ref[q0:kv, :] * log2e
        t = ahead.pop(0)
        if qi + 2 < n_q:
            ahead.append(scores(qi + 2))
        td = jnp.where(causal, t[:, q0:kv], -jnp.inf)
        t = td if qi == 0 else jnp.concatenate([t[:, :q0], td], axis=1)
        mt = jnp.max(t, axis=-1, keepdims=True)
        off = cum_q - (mt + cum_q)
        p = jnp.exp2(t + off)
        denom = jnp.sum(p, axis=-1, keepdims=True)
        o = jnp.dot(p.astype(BF16), v[:kv, :], preferred_element_type=F32)
        o_ref[q0:kv, :] = (o / denom).astype(o_ref.dtype)


def _attention(z3, cum_col, cum_row, q_gain, k_gain, l):
    batch, seq, _ = z3.shape
    depth = q_gain.shape[0]
    hd = ATT_HEAD_DIM
    head = lambda base: pl.BlockSpec((None, seq, hd), lambda b, h: (b, 0, base + h))
    gspec = pl.BlockSpec((None, 1, hd), lambda b, h: (l, 0, 0))
    q_base = 2 * LRU_WIDTH // hd
    return pl.pallas_call(
        functools.partial(_attn_body, seq=seq, tq=Q_TILE),
        grid=(batch, ATT_HEADS),
        in_specs=[head(q_base), head(q_base + ATT_HEADS), head(q_base + 2 * ATT_HEADS),
                  pl.BlockSpec((None, None, seq, 1), lambda b, h: (b, h, 0, 0)),
                  pl.BlockSpec((None, None, 1, seq), lambda b, h: (b, h, 0, 0)),
                  gspec, gspec],
        out_specs=pl.BlockSpec((None, seq, hd), lambda b, h: (b, 0, h)),
        out_shape=jax.ShapeDtypeStruct((batch, seq, ATT_WIDTH), BF16),
        compiler_params=pltpu.CompilerParams(
            dimension_semantics=("arbitrary", "arbitrary"),
            vmem_limit_bytes=_vmem_limit([3 * _nbytes((seq, hd), F32), _nbytes((seq, LANES), F32),
                                          _nbytes((SUBLANES, seq), F32), _nbytes((seq, hd), BF16)],
                                         8 * _nbytes((Q_TILE, seq), F32))),
        name="forgetting_attention",
    )(z3, z3, z3, cum_col, cum_row, q_gain.reshape(depth, 1, hd), k_gain.reshape(depth, 1, hd))


def _router_body(x_ref, g_ref, rw_ref, rb_ref, ri_ref, rwt_ref, cnt_ref, carry_ref, *, bm):
    @pl.when(pl.program_id(0) == 0)
    def _():
        carry_ref[...] = jnp.zeros_like(carry_ref)

    x = x_ref[...]
    h = x * lax.rsqrt(jnp.mean(x * x, axis=-1, keepdims=True) + EPS) * g_ref[...]
    rw = rw_ref[...]
    h_hi, w_hi = h.astype(BF16), rw.astype(BF16)
    h_lo = (h - h_hi.astype(F32)).astype(BF16)
    w_lo = (rw - w_hi.astype(F32)).astype(BF16)
    logits = (jnp.dot(h_hi, w_hi, preferred_element_type=F32)
              + (jnp.dot(h_lo, w_hi, preferred_element_type=F32)
                 + jnp.dot(h_hi, w_lo, preferred_element_type=F32))) + rb_ref[...]
    lane = lax.broadcasted_iota(jnp.int32, (bm, LANES), 1)
    logits = jnp.where(lane < N_EXPERTS, logits, -jnp.inf)
    m1 = jnp.max(logits, axis=-1, keepdims=True)
    i1 = jnp.min(jnp.where(logits == m1, lane, LANES), axis=-1, keepdims=True)
    rest = jnp.where(lane == i1, -jnp.inf, logits)
    m2 = jnp.max(rest, axis=-1, keepdims=True)
    i2 = jnp.min(jnp.where(rest == m2, lane, LANES), axis=-1, keepdims=True)
    e2 = jnp.exp(m2 - m1)
    w1 = 1.0 / (1.0 + e2)
    w2 = e2 / (1.0 + e2)

    onehot = ((lane == i1) | (lane == i2)).astype(BF16)
    row = lax.broadcasted_iota(jnp.int32, (bm, bm), 0)
    col = lax.broadcasted_iota(jnp.int32, (bm, bm), 1)
    tri = (row > col).astype(BF16)
    before = jnp.dot(tri, onehot, preferred_element_type=F32) + carry_ref[0:1, :]
    rank1 = jnp.sum(jnp.where(lane == i1, before, 0.0), axis=-1, keepdims=True).astype(jnp.int32)
    rank2 = jnp.sum(jnp.where(lane == i2, before, 0.0), axis=-1, keepdims=True).astype(jnp.int32)
    total = carry_ref[0:1, :] + jnp.sum(onehot.astype(F32), axis=0, keepdims=True)
    carry_ref[...] = jnp.broadcast_to(total, carry_ref.shape)
    cnt_ref[...] = jnp.broadcast_to(total, cnt_ref.shape)

    ri_ref[...] = jnp.where(lane == 0, i1, jnp.where(lane == 1, i2,
                            jnp.where(lane == 2, rank1, jnp.where(lane == 3, rank2, 0))))
    rwt_ref[...] = jnp.where(lane == 0, w1, jnp.where(lane == 1, w2, 0.0))


def _router(x2, gain, rw, rb, l):
    n, d = x2.shape
    bm = ROW_TILE
    return pl.pallas_call(
        functools.partial(_router_body, bm=bm),
        grid=(n // bm,),
        in_specs=[pl.BlockSpec((bm, d), lambda i: (i, 0)),
                  pl.BlockSpec((None, 1, d), lambda i: (l, 0, 0)),
                  pl.BlockSpec((d, LANES), lambda i: (0, 0)),
                  pl.BlockSpec((1, LANES), lambda i: (0, 0))],
        out_specs=[pl.BlockSpec((bm, LANES), lambda i: (i, 0)),
                   pl.BlockSpec((bm, LANES), lambda i: (i, 0)),
                   pl.BlockSpec((SUBLANES, LANES), lambda i: (0, 0))],
        out_shape=[jax.ShapeDtypeStruct((n, LANES), jnp.int32),
                   jax.ShapeDtypeStruct((n, LANES), F32),
                   jax.ShapeDtypeStruct((SUBLANES, LANES), F32)],
        scratch_shapes=[pltpu.VMEM((SUBLANES, LANES), F32)],
        compiler_params=pltpu.CompilerParams(dimension_semantics=("arbitrary",)),
        name="moe_router",
    )(x2, gain.reshape(gain.shape[0], 1, d), rw, rb)


def _row_copy(src_hbm, dst_vmem, sem, src_row, dst_row):
    return pltpu.make_async_copy(src_hbm.at[pl.ds(src_row, 1), :], dst_vmem.at[pl.ds(dst_row, 1), :], sem)


def _dispatch_body(tok_ref, ng_ref, x_hbm, g_ref, o_ref, buf_ref, sem, *, tg):
    i = pl.program_id(0)
    slot = i & 1
    n_live = ng_ref[0]

    def issue_tile(step, slot_):
        for r in range(tg):
            _row_copy(x_hbm, buf_ref.at[slot_], sem.at[slot_], tok_ref[step * tg + r], r).start(priority=r % 2)

    @pl.when((i == 0) & (n_live > 0))
    def _():
        issue_tile(0, 0)

    @pl.when(i + 1 < n_live)
    def _():
        issue_tile(i + 1, 1 - slot)

    @pl.when(i < n_live)
    def _():
        for r in range(tg):
            _row_copy(x_hbm, buf_ref.at[slot], sem.at[slot], 0, r).wait()
        x = buf_ref[slot]
        h = x * lax.rsqrt(jnp.mean(x * x, axis=-1, keepdims=True) + EPS) * g_ref[...]
        o_ref[...] = h.astype(o_ref.dtype)

    @pl.when(i >= n_live)
    def _():
        o_ref[...] = jnp.zeros_like(o_ref)


def _dispatch(row_token, n_gather_tiles, x2, gain, l, rows):
    n, d = x2.shape
    tg = GATHER_TILE
    return pl.pallas_call(
        functools.partial(_dispatch_body, tg=tg),
        grid_spec=pltpu.PrefetchScalarGridSpec(
            num_scalar_prefetch=2,
            grid=(rows // tg,),
            in_specs=[pl.BlockSpec(memory_space=pl.ANY),
                      pl.BlockSpec((None, 1, d), lambda i, tok, ng: (l, 0, 0))],
            out_specs=pl.BlockSpec((tg, d), lambda i, tok, ng: (i, 0)),
            scratch_shapes=[pltpu.VMEM((2, tg, d), F32), pltpu.SemaphoreType.DMA((2,))]),
        out_shape=jax.ShapeDtypeStruct((rows, d), BF16),
        compiler_params=pltpu.CompilerParams(dimension_semantics=("arbitrary",)),
        name="moe_dispatch_gather",
    )(row_token, n_gather_tiles, x2, gain.reshape(gain.shape[0], 1, d))


N_MOE_TABLES = 7


def _weight_copies(w_hbms, wbuf, sem, m, expert, j, slot, bn):
    col = pl.multiple_of(j * bn, bn)
    return [pltpu.make_async_copy(w.at[m, expert, :, pl.ds(col, bn)], wbuf.at[slot, k], sem.at[slot, k])
            for k, w in enumerate(w_hbms)]


def _moe_step(tables, a_hbm, abuf, asem, w_hbms, wbuf, sem, o_ref, compute, *, tm, bn, m, n_j):
    te_ref, tr_ref, nu_ref, fl_ref, gi_ref, ne_ref, ng_ref = tables
    j, t = pl.program_id(0), pl.program_id(1)
    n_used = nu_ref[0]
    used = t < n_used
    n_groups = ng_ref[0]
    gidx = j * n_groups + gi_ref[t]
    slot = gidx & 1

    item = j * n_used + t
    a_slot = lax.rem(item, MOE_ACT_SLOTS)

    def a_copy(tile, slot_):
        rows = pl.ds(pl.multiple_of(tile * tm, tm), tm)
        return pltpu.make_async_copy(a_hbm.at[rows, :], abuf.at[slot_], asem.at[slot_])

    @pl.when(used)
    def _():
        @pl.when(item == 0)
        def _():
            for w in range(MOE_ACT_SLOTS - 1):
                a_copy(lax.rem(jnp.int32(w), n_used), w).start()
        a_copy(t, a_slot).wait()
        ahead = item + (MOE_ACT_SLOTS - 1)

        @pl.when(ahead < n_j * n_used)
        def _():
            a_copy(lax.rem(ahead, n_used), lax.rem(ahead, MOE_ACT_SLOTS)).start()

    @pl.when(used & (fl_ref[t] == 1))
    def _():
        @pl.when(gidx == 0)
        def _():
            for c in _weight_copies(w_hbms, wbuf, sem, m, te_ref[t], 0, 0, bn):
                c.start()
        for c in _weight_copies(w_hbms, wbuf, sem, m, te_ref[t], j, slot, bn):
            c.wait()
        j_next = j + jnp.where(gi_ref[t] == n_groups - 1, 1, 0)

        @pl.when(j_next < n_j)
        def _():
            for c in _weight_copies(w_hbms, wbuf, sem, m, ne_ref[t], j_next, 1 - slot, bn):
                c.start()

    def run(rows):
        compute(abuf.at[a_slot], rows, slot)
        if rows < tm:
            o_ref[rows:tm, :] = jnp.zeros((tm - rows, o_ref.shape[1]), o_ref.dtype)

    quarter = tm // MOE_TAIL_PARTS
    parts = (tr_ref[t] + (quarter - 1)) // quarter
    for q in range(1, MOE_TAIL_PARTS + 1):
        pl.when(used & (parts == q))(functools.partial(run, q * quarter))

    @pl.when(jnp.logical_not(used))
    def _():
        o_ref[...] = jnp.zeros_like(o_ref)


def _moe_up_body(*refs, **cfg):
    tables = refs[:N_MOE_TABLES]
    a_hbm, wg_hbm, wu_hbm, o_ref, wbuf, sem, abuf, asem = refs[N_MOE_TABLES:]

    def compute(a_ref, rows, slot):
        a = a_ref[0:rows, :]
        g = jnp.dot(a, wbuf[slot, 0].astype(BF16), preferred_element_type=F32)
        u = jnp.dot(a, wbuf[slot, 1].astype(BF16), preferred_element_type=F32)
        o_ref[0:rows, :] = ((g * _sigmoid(g)) * u).astype(o_ref.dtype)

    _moe_step(tables, a_hbm, abuf, asem, (wg_hbm, wu_hbm), wbuf, sem, o_ref, compute, **cfg)


def _moe_down_body(*refs, **cfg):
    tables = refs[:N_MOE_TABLES]
    a_hbm, wd_hbm, o_ref, wbuf, sem, abuf, asem = refs[N_MOE_TABLES:]

    def compute(a_ref, rows, slot):
        o_ref[0:rows, :] = jnp.dot(a_ref[0:rows, :], wbuf[slot, 0].astype(BF16), preferred_element_type=F32)

    _moe_step(tables, a_hbm, abuf, asem, (wd_hbm,), wbuf, sem, o_ref, compute, **cfg)


def _tile_clamp(t, nu):
    return jnp.maximum(jnp.minimum(t, nu[0] - 1), 0)


def _moe_grouped(name, body, tables, a, weights, m, bn, out_dtype, n_acc):
    rows, k = a.shape
    n_out = weights[0].shape[-1]
    tm = MOE_TILE
    n_j = n_out // bn
    o_map = lambda j, t, te, tr, nu, fl, gi, ne, ng: (t, j)
    blocks = [_nbytes((tm, bn), out_dtype)]
    wbuf_bytes = 2 * len(weights) * _nbytes((k, bn), F32) + MOE_ACT_SLOTS * _nbytes((tm, k), BF16)
    temps = len(weights) * _nbytes((k, bn), BF16) + n_acc * _nbytes((tm, bn), F32)
    return pl.pallas_call(
        functools.partial(body, tm=tm, bn=bn, m=m, n_j=n_j),
        grid_spec=pltpu.PrefetchScalarGridSpec(
            num_scalar_prefetch=N_MOE_TABLES,
            grid=(n_j, rows // tm),
            in_specs=[pl.BlockSpec(memory_space=pl.ANY)] * (1 + len(weights)),
            out_specs=pl.BlockSpec((tm, bn), o_map),
            scratch_shapes=[pltpu.VMEM((2, len(weights), k, bn), F32),
                            pltpu.SemaphoreType.DMA((2, len(weights))),
                            pltpu.VMEM((MOE_ACT_SLOTS, tm, k), BF16),
                            pltpu.SemaphoreType.DMA((MOE_ACT_SLOTS,))]),
        out_shape=jax.ShapeDtypeStruct((rows, n_out), out_dtype),
        compiler_params=pltpu.CompilerParams(dimension_semantics=("arbitrary", "arbitrary"),
                                             vmem_limit_bytes=_vmem_limit(blocks, temps, wbuf_bytes)),
        name=name,
    )(*tables, a, *weights)


def _moe_up(tables, xs, wg, wu, m):
    return _moe_grouped("moe_up_swiglu", _moe_up_body, tables, xs, (wg, wu), m, 1024, BF16, 3)


def _moe_down(tables, act, wd, m):
    return _moe_grouped("moe_down", _moe_down_body, tables, act, (wd,), m, 512, F32, 2)


def _combine_body(pos_ref, y_hbm, x_ref, w_ref, o_ref, buf_ref, sem, *, tc, n_steps):
    i = pl.program_id(0)
    slot = i & 1

    def issue_tile(step, slot_):
        base = step * (tc * TOP_K)
        for r in range(tc):
            for k in range(TOP_K):
                _row_copy(y_hbm, buf_ref.at[slot_, k], sem.at[slot_],
                          pos_ref[base + r * TOP_K + k], r).start(priority=k)

    @pl.when(i == 0)
    def _():
        issue_tile(0, 0)

    @pl.when(i + 1 < n_steps)
    def _():
        issue_tile(i + 1, 1 - slot)

    for r in range(tc):
        for k in range(TOP_K):
            _row_copy(y_hbm, buf_ref.at[slot, k], sem.at[slot], 0, r).wait()
    w = w_ref[...]
    o_ref[...] = x_ref[...] + w[:, 0:1] * buf_ref[slot, 0] + w[:, 1:2] * buf_ref[slot, 1]


def _combine(pos_flat, y, x2, route_w):
    n, d = x2.shape
    tc = GATHER_TILE
    return pl.pallas_call(
        functools.partial(_combine_body, tc=tc, n_steps=n // tc),
        grid_spec=pltpu.PrefetchScalarGridSpec(
            num_scalar_prefetch=1,
            grid=(n // tc,),
            in_specs=[pl.BlockSpec(memory_space=pl.ANY),
                      pl.BlockSpec((tc, d), lambda i, pos: (i, 0)),
                      pl.BlockSpec((tc, LANES), lambda i, pos: (i, 0))],
            out_specs=pl.BlockSpec((tc, d), lambda i, pos: (i, 0)),
            scratch_shapes=[pltpu.VMEM((2, TOP_K, tc, d), F32), pltpu.SemaphoreType.DMA((2,))]),
        out_shape=jax.ShapeDtypeStruct((n, d), F32),
        compiler_params=pltpu.CompilerParams(
            dimension_semantics=("arbitrary",),
            vmem_limit_bytes=_vmem_limit([2 * _nbytes((tc, d), F32), _nbytes((tc, LANES), F32)],
                                         2 * TOP_K * _nbytes((tc, d), F32))),
        name="moe_combine_gather",
    )(pos_flat, y, x2, route_w)


def _routing_tables(route_i, counts, n_tokens, rows):
    tm = MOE_TILE
    expert = route_i[:, 0:TOP_K]
    rank = route_i[:, TOP_K:2 * TOP_K]
    cnt = counts[0, :N_EXPERTS].astype(jnp.int32)
    tiles_e = (cnt + (tm - 1)) // tm
    tile_end = jnp.cumsum(tiles_e)
    row_off = (tile_end - tiles_e) * tm
    pos = row_off[expert] + rank
    token = jnp.broadcast_to(jnp.arange(n_tokens, dtype=jnp.int32)[:, None], pos.shape)
    row_token = jnp.zeros((rows,), jnp.int32).at[pos.reshape(-1)].set(token.reshape(-1), unique_indices=True)
    tiles = jnp.arange(rows // tm, dtype=jnp.int32)
    tile_expert = jnp.minimum(jnp.sum(tiles[:, None] >= tile_end[None, :], axis=1),
                              N_EXPERTS - 1).astype(jnp.int32)
    tile_start = tile_end - tiles_e
    tile_rows = jnp.clip(cnt[tile_expert] - (tiles - tile_start[tile_expert]) * tm, 0, tm)
    n_used = tile_end[N_EXPERTS - 1:N_EXPERTS].astype(jnp.int32)
    experts = jnp.arange(N_EXPERTS, dtype=jnp.int32)
    nonempty = tiles_e > 0
    later = jnp.where(nonempty[None, :] & (experts[None, :] > experts[:, None]), experts[None, :], N_EXPERTS)
    first_group = jnp.min(jnp.where(nonempty, experts, N_EXPERTS))
    next_e = jnp.min(later, axis=1)
    next_e = jnp.where(next_e == N_EXPERTS, first_group, next_e)
    group_of_e = jnp.cumsum(nonempty.astype(jnp.int32)) - 1
    tile_first = (tiles == tile_start[tile_expert]).astype(jnp.int32)
    n_groups = jnp.sum(nonempty.astype(jnp.int32)).reshape(1)
    i32 = lambda v: v.astype(jnp.int32)
    tables = (tile_expert, i32(tile_rows), n_used, tile_first, i32(group_of_e[tile_expert]),
              i32(next_e[tile_expert]), i32(n_groups))
    return pos.reshape(-1).astype(jnp.int32), row_token, tables


def _moe_ffn(x2, gain, l, router_w, router_b, wg, wu, wd, m):
    n, d = x2.shape
    rows = n * TOP_K + N_EXPERTS * MOE_TILE
    rw = jnp.pad(router_w[m], ((0, 0), (0, LANES - N_EXPERTS)))
    rb = jnp.pad(router_b[m], (0, LANES - N_EXPERTS)).reshape(1, LANES)
    route_i, route_w, counts = _router(x2, gain, rw, rb, l)
    pos_flat, row_token, tables = _routing_tables(route_i, counts, n, rows)
    n_used = tables[2]
    xs = _dispatch(row_token, n_used * (MOE_TILE // GATHER_TILE), x2, gain, l, rows)
    act = _moe_up(tables, xs, wg, wu, m)
    y = _moe_down(tables, act, wd, m)
    return _combine(pos_flat, y, x2, route_w)


def _dense_ffn(x2, h, wg, wu, wd, m):
    n, d = x2.shape
    dff = wg.shape[-1]
    bm, bn = 1024, 512
    act = _mm("ffn_up_swiglu", n, dff, bm, bn,
              [(h, _a_spec(bm, d))],
              [(wg, _w_spec(d, bn, m)), (wu, _w_spec(d, bn, m))], [],
              [(0, 0), (0, 1)], _ep_swiglu, BF16)
    bm, bn = 512, 512
    return _mm("ffn_down_residual", n, d, bm, bn,
               [(act, _a_spec(bm, dff))],
               [(wd, _w_spec(dff, bn, m))],
               [(x2, pl.BlockSpec((bm, bn), lambda j, i: (i, j)))],
               [(0, 0)], _ep_residual, F32)


def _mixer(x2, l, batch, seq, norm_mix_g, w_in, lru_conv_w, lru_conv_b, lru_wa, lru_ba, lru_wx, lru_bx,
           lru_lambda, fox_bf, q_norm_g, k_norm_g, sc_conv_w, w_branch_lru, w_branch_att, w_branch_conv,
           w_merge, b_merge, w_out, ffn_gain):
    n, d = x2.shape
    depth = w_in.shape[0]
    u = _rmsnorm(x2, norm_mix_g, l)

    w_t = jnp.swapaxes(w_in, 1, 2)
    w_tail_t = w_t[l, COL_TAIL:, :][None]
    t_spec = lambda layer: pl.BlockSpec((None, bn, d), lambda j, i: (layer, j, 0))
    bm, bn = 1024, 1024
    z = _mm("in_proj_main", n, COL_MAIN, bm, bn, [(u, _a_spec(bm, d))], [(w_t, t_spec(l))], [],
            [(0, 0)], _ep_identity, F32, w_rows_are_outputs=True)
    zt = _mm("in_proj_tail", n, 3 * CONV_WIDTH, bm, bn, [(u, _a_spec(bm, d))],
             [(w_tail_t, t_spec(0))], [], [(0, 0)], _ep_identity, F32, w_rows_are_outputs=True)
    z3 = z.reshape(batch, seq, COL_MAIN)
    zt3 = zt.reshape(batch, seq, 3 * CONV_WIDTH)

    wf = jnp.pad(w_t[l, COL_F:COL_TAIL, :], ((0, LANES - ATT_HEADS), (0, 0)))
    bf = jnp.pad(fox_bf[l], (0, LANES - ATT_HEADS)).reshape(1, LANES)
    cum = _fgate(u, wf, bf, batch, seq)[:, :ATT_HEADS].reshape(batch, seq, ATT_HEADS)
    cum_h = cum.transpose(0, 2, 1)
    cum_col = cum_h.reshape(batch, ATT_HEADS, seq, 1)
    cum_row = cum_h.reshape(batch, ATT_HEADS, 1, seq)

    y_lru, y_conv = _conv_branches(z3, zt3, lru_conv_w, lru_conv_b, lru_wa, lru_ba, lru_wx, lru_bx, lru_lambda,
                                   sc_conv_w, l)
    y_att = _attention(z3, cum_col, cum_row, q_norm_g, k_norm_g, l)

    bm, bn = 1024, 256
    nb = d // bn
    ws = _w_spec
    merged = _mm(
        "gated_merge", n, d, bm, bn,
        [(u, _a_spec(bm, d)), (y_lru.reshape(n, LRU_WIDTH), _a_spec(bm, LRU_WIDTH)),
         (y_att.reshape(n, ATT_WIDTH), _a_spec(bm, ATT_WIDTH)),
         (y_conv.reshape(n, CONV_WIDTH), _a_spec(bm, CONV_WIDTH))],
        [(w_merge, ws(d, bn, l, 0)), (w_merge, ws(d, bn, l, nb)), (w_merge, ws(d, bn, l, 2 * nb)),
         (w_branch_lru, ws(LRU_WIDTH, bn, l)), (w_branch_att, ws(ATT_WIDTH, bn, l)),
         (w_branch_conv, ws(CONV_WIDTH, bn, l))],
        [(b_merge.reshape(depth, 1, -1), _row_spec(bn, l, g * nb)) for g in range(3)],
        [(0, 0), (0, 1), (0, 2), (1, 3), (2, 4), (3, 5)], _ep_merge, BF16)

    return _out_proj(merged, w_out, x2, l, ffn_gain)


def _out_proj_body(a_ref, w_ref, x_ref, *rest):
    xn = x_ref[...] + jnp.dot(a_ref[...], w_ref[...].astype(BF16), preferred_element_type=F32)
    if len(rest) == 1:
        rest[0][...] = xn
    else:
        g_ref, o_ref, h_ref = rest
        o_ref[...] = xn
        h = xn * lax.rsqrt(jnp.mean(xn * xn, axis=-1, keepdims=True) + EPS) * g_ref[...]
        h_ref[...] = h.astype(h_ref.dtype)


def _out_proj(merged, w_out, x2, l, ffn_gain):
    n, d = x2.shape
    bm = 256
    row = pl.BlockSpec((bm, d), lambda i: (i, 0))
    in_specs = [row, pl.BlockSpec((None, d, d), lambda i: (l, 0, 0), pipeline_mode=pl.Buffered(1)), row]
    args = [merged, w_out, x2]
    out_specs, out_shape = row, jax.ShapeDtypeStruct((n, d), F32)
    if ffn_gain is not None:
        in_specs.append(pl.BlockSpec((None, 1, d), lambda i: (l, 0, 0)))
        args.append(ffn_gain.reshape(ffn_gain.shape[0], 1, d))
        out_specs = [row, row]
        out_shape = [out_shape, jax.ShapeDtypeStruct((n, d), BF16)]
    blocks = [_nbytes((bm, d), BF16), 2 * _nbytes((bm, d), F32), _nbytes((bm, d), BF16)]
    return pl.pallas_call(
        _out_proj_body,
        grid=(n // bm,),
        in_specs=in_specs,
        out_specs=out_specs,
        out_shape=out_shape,
        compiler_params=pltpu.CompilerParams(
            dimension_semantics=("arbitrary",),
            vmem_limit_bytes=_vmem_limit(blocks, _nbytes((d, d), BF16) + 2 * _nbytes((bm, d), F32),
                                         _nbytes((d, d), F32))),
        name="out_proj_residual",
    )(*args)


def kernel(x, norm_mix_g, w_in, lru_conv_w, lru_conv_b, lru_wa, lru_ba, lru_wx, lru_bx, lru_lambda, fox_bf, q_norm_g, k_norm_g, sc_conv_w, w_branch_lru, w_branch_att, w_branch_conv, w_merge, b_merge, w_out, norm_ffn_g, ffn_wg, ffn_wu, ffn_wd, router_w, router_b, moe_wg, moe_wu, moe_wd):
    batch, seq, d = x.shape
    depth = w_in.shape[0]
    x2 = x.reshape(batch * seq, d)
    for l in range(depth):
        dense = l % 2 == 0
        mixed = _mixer(x2, l, batch, seq, norm_mix_g, w_in, lru_conv_w, lru_conv_b, lru_wa, lru_ba, lru_wx,
                       lru_bx, lru_lambda, fox_bf, q_norm_g, k_norm_g, sc_conv_w, w_branch_lru, w_branch_att,
                       w_branch_conv, w_merge, b_merge, w_out, norm_ffn_g if dense else None)
        if dense:
            x2, h = mixed
            x2 = _dense_ffn(x2, h, ffn_wg, ffn_wu, ffn_wd, l // 2)
        else:
            x2 = _moe_ffn(mixed, norm_ffn_g, l, router_w, router_b, moe_wg, moe_wu, moe_wd, l // 2)
    return x2.reshape(batch, seq, d)
```
